```python
import math
import jax, jax.numpy as jnp
from jax import lax
import numpy as np

D_MODEL = 2048
BATCH = 4
SEQ = 4096
DEPTH = 4

H_A = 8
DH_A = 64
DILATED_PATTERNS = ((128, 1), (512, 4), (2048, 16))
H_K_B = 4
H_V_B = 8
DK_B = 128
DV_B = 128
CONV_K = 4
DN_CHUNK = 64
H_C = 8
HKV_C = 2
GRP_C = H_C // HKV_C
DH_C = 64
SWA_WINDOW = 128
BLOCK = 128
A_W = H_A * DH_A
BK_W = H_K_B * DK_B
BV_W = H_V_B * DV_B
C_Q_W = H_C * DH_C
C_KV_W = HKV_C * DH_C
CONV_CH = 2 * BK_W + BV_W
MIX = A_W + BV_W + C_Q_W
N_IN = 3 * A_W + 2 * BK_W + 2 * BV_W + 2 * H_V_B + C_Q_W + 2 * C_KV_W
N_GROUPS = 4
EXPERTS_PER_GROUP = 8
N_EXPERTS = N_GROUPS * EXPERTS_PER_GROUP
TOP_K = 2
D_FF = 512
MOE_BLOCK = 128
EPS = 1e-6

kernel_name = 'hybrid_parallel_heads_hier_moe'


def rmsnorm(x, g):
    xf = x.astype(jnp.float32)
    y = xf * lax.rsqrt(jnp.mean(xf * xf, axis=-1, keepdims=True) + EPS)
    return y.astype(x.dtype) * g


def l2norm(x):
    return x * lax.rsqrt(jnp.sum(x * x, axis=-1, keepdims=True) + EPS)


def alibi_slopes(n):
    return 2.0 ** (-8.0 * jnp.arange(1, n + 1, dtype=jnp.float32) / n)


def banded_attention(q, k, v, max_dist, slopes, dist_scale, sinks=None):
    N, G, R, L, dh = q.shape
    nb = -(-L // BLOCK)
    pad = nb * BLOCK - L
    qb = jnp.pad(q, ((0, 0), (0, 0), (0, 0), (0, pad), (0, 0))).reshape(N, G, R, nb, BLOCK, dh)
    kp = jnp.pad(k, ((0, 0), (0, 0), (BLOCK, pad), (0, 0))).reshape(N, G, nb + 1, BLOCK, dh)
    vp = jnp.pad(v, ((0, 0), (0, 0), (BLOCK, pad), (0, 0))).reshape(N, G, nb + 1, BLOCK, dh)
    kb = jnp.concatenate([kp[:, :, :-1], kp[:, :, 1:]], axis=3)
    vb = jnp.concatenate([vp[:, :, :-1], vp[:, :, 1:]], axis=3)
    s = jnp.einsum('ngrbqd,ngbkd->ngrbqk', qb, kb, preferred_element_type=jnp.float32) * (dh ** -0.5)
    rel = (jnp.arange(BLOCK)[:, None] + BLOCK) - jnp.arange(2 * BLOCK)[None, :]
    kabs = jnp.arange(nb)[:, None] * BLOCK - BLOCK + jnp.arange(2 * BLOCK)[None, :]
    valid = ((rel >= 0) & (rel <= max_dist))[None] & (kabs >= 0)[:, None, :]
    bias = -(slopes.astype(jnp.float32) * dist_scale)[None, :, :, None, None, None] * rel.astype(jnp.float32)
    s = jnp.where(valid, s + bias, -jnp.inf)
    m = jnp.max(s, axis=-1, keepdims=True)
    if sinks is not None:
        sk = sinks.astype(jnp.float32)[None, :, :, None, None, None]
        m = jnp.maximum(m, sk)
    p = jnp.exp(s - m)
    denom = jnp.sum(p, axis=-1, keepdims=True)
    if sinks is not None:
        denom = denom + jnp.exp(sk - m)
    o = jnp.einsum('ngrbqk,ngbkd->ngrbqd', p, vb.astype(jnp.float32)) / denom
    lse = (m + jnp.log(denom))[..., 0]
    o = o.reshape(N, G, R, nb * BLOCK, dh)[:, :, :, :L].astype(v.dtype)
    lse = lse.reshape(N, G, R, nb * BLOCK)[..., :L]
    return o, lse


def dilated_branch(q, k, v, window, dilation, slopes):
    B, S, H, dh = q.shape
    Ld = S // dilation

    def split(t):
        return t.reshape(B, Ld, dilation, H, dh).transpose(0, 2, 3, 1, 4).reshape(B * dilation, H, Ld, dh)

    o, lse = banded_attention(split(q)[:, :, None], split(k), split(v), window // dilation,
                              slopes[:, None], float(dilation))
    o = o[:, :, 0].reshape(B, dilation, H, Ld, dh).transpose(0, 3, 1, 2, 4).reshape(B, S, H, dh)
    lse = lse[:, :, 0].reshape(B, dilation, H, Ld).transpose(0, 3, 1, 2).reshape(B, S, H)
    return o, lse


def causal_depthwise_conv(x, w):
    K, C = w.shape
    return lax.conv_general_dilated(x, w[:, None, :].astype(x.dtype), window_strides=(1,),
                                    padding=[(K - 1, 0)], dimension_numbers=('NWC', 'WIO', 'NWC'),
                                    feature_group_count=C)


def gated_delta_rule(q, k, v, g, beta):
    B, H, S, dk = q.shape
    dv = v.shape[-1]
    C = DN_CHUNK
    n = S // C
    q = q * (dk ** -0.5)

    def chunks(t):
        return t.reshape(B, H, n, C, *t.shape[3:])

    q, k, v, g, beta = chunks(q), chunks(k), chunks(v), chunks(g), chunks(beta)
    gc = jnp.cumsum(g, axis=-1)
    incl = jnp.tril(jnp.ones((C, C), bool))
    strict = jnp.tril(jnp.ones((C, C), bool), -1)
    decay = jnp.exp(jnp.where(incl, gc[..., :, None] - gc[..., None, :], -jnp.inf))
    kb = k * beta[..., None]
    lmat = jnp.where(strict, jnp.einsum('bhnid,bhnjd->bhnij', kb, k) * decay, 0.0)
    a_mat = lmat + jnp.eye(C, dtype=lmat.dtype)
    rhs = jnp.concatenate([v * beta[..., None], kb * jnp.exp(gc)[..., None]], axis=-1)
    sol = lax.linalg.triangular_solve(a_mat, rhs, left_side=True, lower=True, unit_diagonal=True)
    u, w = sol[..., :dv], sol[..., dv:]
    qk = jnp.where(incl, jnp.einsum('bhnid,bhnjd->bhnij', q, k) * decay, 0.0)
    q_dec = q * jnp.exp(gc)[..., None]
    k_dec = k * jnp.exp(gc[..., -1:] - gc)[..., None]
    g_last = jnp.exp(gc[..., -1])

    def step(state, xs):
        q_i, k_i, u_i, w_i, qk_i, gl_i = xs
        v_new = u_i - jnp.einsum('bhck,bhkv->bhcv', w_i, state)
        o = jnp.einsum('bhck,bhkv->bhcv', q_i, state) + jnp.einsum('bhcj,bhjv->bhcv', qk_i, v_new)
        state = state * gl_i[..., None, None] + jnp.einsum('bhck,bhcv->bhkv', k_i, v_new)
        return state, o

    xs = tuple(jnp.moveaxis(t, 2, 0) for t in (q_dec, k_dec, u, w, qk, g_last))
    _, o = lax.scan(step, jnp.zeros((B, H, dk, dv), jnp.float32), xs)
    return jnp.moveaxis(o, 0, 2).reshape(B, H, S, dv)


def gated_deltanet(bq, bk, bv, bz, bb, ba, conv_w, a_log, dt_bias, norm_g):
    B, S, _ = bq.shape
    qkv = jax.nn.silu(causal_depthwise_conv(jnp.concatenate([bq, bk, bv], axis=-1), conv_w)).astype(jnp.float32)
    q, k, v = jnp.split(qkv, [BK_W, 2 * BK_W], axis=-1)
    rep = H_V_B // H_K_B
    q = jnp.repeat(l2norm(q.reshape(B, S, H_K_B, DK_B)), rep, axis=2)
    k = jnp.repeat(l2norm(k.reshape(B, S, H_K_B, DK_B)), rep, axis=2)
    v = v.reshape(B, S, H_V_B, DV_B)
    beta = jax.nn.sigmoid(bb.astype(jnp.float32))
    g = -jnp.exp(a_log.astype(jnp.float32)) * jax.nn.softplus(ba.astype(jnp.float32) + dt_bias.astype(jnp.float32))
    o = gated_delta_rule(jnp.moveaxis(q, 2, 1), jnp.moveaxis(k, 2, 1), jnp.moveaxis(v, 2, 1),
                         jnp.moveaxis(g, 2, 1), jnp.moveaxis(beta, 2, 1))
    o = jnp.moveaxis(o, 1, 2)
    o = rmsnorm(o, norm_g.astype(jnp.float32)) * jax.nn.silu(bz.astype(jnp.float32).reshape(B, S, H_V_B, DV_B))
    return o.reshape(B, S, BV_W).astype(bq.dtype)


def hybrid_mixer(h, w_in, conv_w, a_log, dt_bias, dn_norm_g, sinks, w_out):
    B, S, _ = h.shape
    proj = h @ w_in
    sizes = [A_W, A_W, A_W, BK_W, BK_W, BV_W, BV_W, H_V_B, H_V_B, C_Q_W, C_KV_W, C_KV_W]
    cuts = [int(i) for i in np.cumsum(sizes)[:-1]]
    aq, ak, av, bq, bk, bv, bz, bb, ba, cq, ck, cv = jnp.split(proj, cuts, axis=-1)

    aq, ak, av = (t.reshape(B, S, H_A, DH_A) for t in (aq, ak, av))
    slopes_a = alibi_slopes(H_A)
    outs, lses = zip(*[dilated_branch(aq, ak, av, wdw, dil, slopes_a) for (wdw, dil) in DILATED_PATTERNS])
    wts = jax.nn.softmax(jnp.stack(lses, axis=0), axis=0)
    o_a = jnp.einsum('pbsh,pbshd->bshd', wts, jnp.stack(outs, axis=0).astype(jnp.float32))
    o_a = o_a.astype(h.dtype).reshape(B, S, A_W)

    o_b = gated_deltanet(bq, bk, bv, bz, bb, ba, conv_w, a_log, dt_bias, dn_norm_g)

    cq = cq.reshape(B, S, HKV_C, GRP_C, DH_C).transpose(0, 2, 3, 1, 4)
    ck = ck.reshape(B, S, HKV_C, DH_C).transpose(0, 2, 1, 3)
    cv = cv.reshape(B, S, HKV_C, DH_C).transpose(0, 2, 1, 3)
    o_c, _ = banded_attention(cq, ck, cv, SWA_WINDOW - 1, alibi_slopes(H_C).reshape(HKV_C, GRP_C), 1.0,
                              sinks.reshape(HKV_C, GRP_C))
    o_c = o_c.transpose(0, 3, 1, 2, 4).reshape(B, S, C_Q_W)

    return jnp.concatenate([o_a, o_b, o_c], axis=-1) @ w_out


def grouped_expert_mlp(xt, eidx, w_gate, w_up, w_down):
    T, K = eidx.shape
    E = w_gate.shape[0]
    A = T * K
    flat_e = eidx.reshape(A)
    order = jnp.argsort(flat_e)
    sorted_e = flat_e[order]
    counts = jnp.bincount(flat_e, length=E)
    starts = jnp.cumsum(counts) - counts
    padded = (counts + MOE_BLOCK - 1) // MOE_BLOCK * MOE_BLOCK
    pad_ends = jnp.cumsum(padded)
    pad_starts = pad_ends - padded
    dest = pad_starts[sorted_e] + jnp.arange(A) - starts[sorted_e]
    nblk = -(-A // MOE_BLOCK) + E
    rows = jnp.zeros((nblk * MOE_BLOCK, xt.shape[1]), xt.dtype).at[dest].set(xt[order // K])
    blk_expert = jnp.minimum(jnp.searchsorted(pad_ends, jnp.arange(nblk) * MOE_BLOCK, side='right'), E - 1)

    def expert_block(args):
        xb, e = args
        hid = jax.nn.silu(xb @ w_gate[e]) * (xb @ w_up[e])
        return hid @ w_down[e]

    yb = lax.map(expert_block, (rows.reshape(nblk, MOE_BLOCK, -1), blk_expert))
    y_sorted = yb.reshape(nblk * MOE_BLOCK, -1)[dest]
    y = jnp.zeros((A, y_sorted.shape[-1]), y_sorted.dtype).at[order].set(y_sorted)
    return y.reshape(T, K, -1)


def hierarchical_moe(h, rg_w, rg_b, re_w, re_b, w_gate, w_up, w_down):
    B, S, D = h.shape
    xt = h.reshape(-1, D)
    T = xt.shape[0]
    tok = jnp.arange(T)
    glog = (xt @ rg_w).astype(jnp.float32) + rg_b.astype(jnp.float32)
    gsel = jnp.argmax(glog, axis=-1)
    pg = jax.nn.softmax(glog, axis=-1)[tok, gsel]
    elog = ((xt @ re_w).astype(jnp.float32) + re_b.astype(jnp.float32)).reshape(T, N_GROUPS, EXPERTS_PER_GROUP)
    top_v, top_i = lax.top_k(elog[tok, gsel], TOP_K)
    gates = pg[:, None] * jax.nn.softmax(top_v, axis=-1)
    eidx = gsel[:, None] * EXPERTS_PER_GROUP + top_i
    y = grouped_expert_mlp(xt, eidx, w_gate, w_up, w_down)
    return jnp.einsum('tk,tkd->td', gates.astype(y.dtype), y).reshape(B, S, D)


def setup_inputs(seed: int = 0) -> dict:
    key = jax.random.key(seed)
    ks = jax.random.split(key, 21)
    L, D = DEPTH, D_MODEL

    def nrm(k, shape, scale):
        return jax.random.normal(k, shape, jnp.float32) * scale

    dt = jnp.exp(jax.random.uniform(ks[9], (L, H_V_B), jnp.float32, math.log(1e-3), math.log(1e-1)))
    return {
        'x': nrm(ks[0], (BATCH, SEQ, D), 1.0),
        'c': nrm(ks[1], (BATCH, D), 1.0),
        'norm1_g': 1.0 + nrm(ks[2], (L, D), 0.02),
        'norm2_g': 1.0 + nrm(ks[3], (L, D), 0.02),
        'ada_w': nrm(ks[4], (L, D, 6 * D), 0.5 * D ** -0.5),
        'ada_b': nrm(ks[5], (L, 6 * D), 0.02),
        'w_in': nrm(ks[6], (L, D, N_IN), D ** -0.5),
        'dn_conv_w': nrm(ks[7], (L, CONV_K, CONV_CH), CONV_K ** -0.5),
        'dn_a_log': jnp.log(jax.random.uniform(ks[8], (L, H_V_B), jnp.float32, 1.0, 16.0)),
        'dn_dt_bias': dt + jnp.log(-jnp.expm1(-dt)),
        'dn_norm_g': 1.0 + nrm(ks[10], (L, DV_B), 0.02),
        'attn_sinks': nrm(ks[11], (L, H_C), 1.0),
        'w_out': nrm(ks[12], (L, MIX, D), MIX ** -0.5),
        'router_group_w': nrm(ks[13], (L, D, N_GROUPS), D ** -0.5),
        'router_group_b': nrm(ks[14], (L, N_GROUPS), 0.01),
        'router_expert_w': nrm(ks[15], (L, D, N_EXPERTS), D ** -0.5),
        'router_expert_b': nrm(ks[16], (L, N_EXPERTS), 0.01),
        'expert_w_gate': nrm(ks[17], (L, N_EXPERTS, D, D_FF), D ** -0.5),
        'expert_w_up': nrm(ks[18], (L, N_EXPERTS, D, D_FF), D ** -0.5),
        'expert_w_down': nrm(ks[19], (L, N_EXPERTS, D_FF, D), D_FF ** -0.5),
        'final_norm_g': 1.0 + nrm(ks[20], (D,), 0.02),
    }


def reference(x, c, norm1_g, norm2_g, ada_w, ada_b, w_in, dn_conv_w, dn_a_log, dn_dt_bias, dn_norm_g,
              attn_sinks, w_out, router_group_w, router_group_b, router_expert_w, router_expert_b,
              expert_w_gate, expert_w_up, expert_w_down, final_norm_g):
    c_act = jax.nn.silu(c)
    for l in range(DEPTH):
        mod = (c_act @ ada_w[l] + ada_b[l])[:, None, :]
        sh1, sc1, g1, sh2, sc2, g2 = jnp.split(mod, 6, axis=-1)
        h = rmsnorm(x, norm1_g[l]) * (1.0 + sc1) + sh1
        x = x + g1 * hybrid_mixer(h, w_in[l], dn_conv_w[l], dn_a_log[l], dn_dt_bias[l], dn_norm_g[l],
                                  attn_sinks[l], w_out[l])
        h = rmsnorm(x, norm2_g[l]) * (1.0 + sc2) + sh2
        x = x + g2 * hierarchical_moe(h, router_group_w[l], router_group_b[l], router_expert_w[l],
                                      router_expert_b[l], expert_w_gate[l], expert_w_up[l], expert_w_down[l])
    return rmsnorm(x, final_norm_g)
```

```python
import functools
import math

import jax
import jax.numpy as jnp
from jax import lax
from jax.experimental import pallas as pl
from jax.experimental.pallas import tpu as pltpu

F32 = jnp.float32
BF16 = jnp.bfloat16
HIGHEST = lax.Precision.HIGHEST

LANES = 128
SUBLANES = 8
MIB = 1024 * 1024

EPS = 1e-6
D_MODEL = 2048
H_A, DH_A = 8, 64
DILATED_PATTERNS = ((128, 1), (512, 4), (2048, 16))
H_K_B, H_V_B, DK_B, DV_B, CONV_K = 4, 8, 128, 128, 4
H_C, HKV_C, DH_C, SWA_WINDOW = 8, 2, 64, 128
ATT_BLOCK = 128
A_W = H_A * DH_A
BK_W = H_K_B * DK_B
BV_W = H_V_B * DV_B
C_Q_W = H_C * DH_C
C_KV_W = HKV_C * DH_C
CONV_CH = 2 * BK_W + BV_W
PA_W = 3 * A_W
PB_GATE_OFF = CONV_CH + BV_W
PB_W = PB_GATE_OFF + LANES
PC_W = C_Q_W + 2 * C_KV_W
N_PERM = PA_W + PB_W + PC_W
N_GROUPS, EXPERTS_PER_GROUP, TOP_K, D_FF = 4, 8, 2, 512
N_EXPERTS = N_GROUPS * EXPERTS_PER_GROUP
DN_CHUNK = 128


def _cparams(semantics, vmem_mib):
    return pltpu.CompilerParams(dimension_semantics=semantics, vmem_limit_bytes=vmem_mib * MIB)


def _silu(x):
    return x * jax.nn.sigmoid(x)


def _ada_kernel(c_ref, w_ref, b_ref, o_ref):
    c = c_ref[...]
    ca = _silu(c).astype(BF16)
    o_ref[0] = jnp.dot(ca, w_ref[0].astype(BF16), preferred_element_type=F32) + b_ref[0]


def ada_modulation(c, ada_w, ada_b):
    L, D, N = ada_w.shape
    B = c.shape[0]
    cp = jnp.zeros((SUBLANES, D), F32).at[:B].set(c)
    tn = 1024
    return pl.pallas_call(
        _ada_kernel,
        grid=(L, N // tn),
        in_specs=[pl.BlockSpec((SUBLANES, D), lambda l, j: (0, 0)),
                  pl.BlockSpec((1, D, tn), lambda l, j: (l, 0, j)),
                  pl.BlockSpec((1, 1, tn), lambda l, j: (l, 0, j))],
        out_specs=pl.BlockSpec((1, SUBLANES, tn), lambda l, j: (l, 0, j)),
        out_shape=jax.ShapeDtypeStruct((L, SUBLANES, N), F32),
        compiler_params=_cparams(("arbitrary", "arbitrary"), 40),
        name="ada_modulation",
    )(cp, ada_w, ada_b.reshape(L, 1, N))


def _inproj_kernel(x_ref, g_ref, sc_ref, sh_ref, w_ref, oa_ref, ob_ref, oc_ref):
    x = x_ref[...]
    y = x * lax.rsqrt(jnp.mean(x * x, axis=-1, keepdims=True) + EPS) * g_ref[...]
    hb = (y * (1.0 + sc_ref[0]) + sh_ref[0]).astype(BF16)
    oa_ref[...] = jnp.dot(hb, w_ref[:, 0:PA_W], preferred_element_type=F32).astype(BF16)
    ob_ref[...] = jnp.dot(hb, w_ref[:, PA_W:PA_W + PB_W], preferred_element_type=F32)
    oc_ref[...] = jnp.dot(hb, w_ref[:, PA_W + PB_W:N_PERM], preferred_element_type=F32).astype(BF16)


def input_projection(x, norm_g, mod, w_perm, seq):
    T, D = x.shape
    tm = 256
    per_b = seq // tm
    return pl.pallas_call(
        _inproj_kernel,
        grid=(T // tm,),
        in_specs=[pl.BlockSpec((tm, D), lambda i: (i, 0)),
                  pl.BlockSpec((1, D), lambda i: (0, 0)),
                  pl.BlockSpec((1, 1, D), lambda i: (i // per_b, 0, 1)),
                  pl.BlockSpec((1, 1, D), lambda i: (i // per_b, 0, 0)),
                  pl.BlockSpec((D, N_PERM), lambda i: (0, 0), pipeline_mode=pl.Buffered(1))],
        out_specs=[pl.BlockSpec((tm, PA_W), lambda i: (i, 0)),
                   pl.BlockSpec((tm, PB_W), lambda i: (i, 0)),
                   pl.BlockSpec((tm, PC_W), lambda i: (i, 0))],
        out_shape=[jax.ShapeDtypeStruct((T, PA_W), BF16),
                   jax.ShapeDtypeStruct((T, PB_W), F32),
                   jax.ShapeDtypeStruct((T, PC_W), BF16)],
        compiler_params=_cparams(("arbitrary",), 56),
        name="input_projection",
    )(x, norm_g.reshape(1, D), mod, mod, w_perm)


def _attn_kernel(*refs, n_heads, grp, dh, scale, has_sink, want_lse):
    q_ref, kp_ref, kc_ref, vp_ref, vc_ref, bias_ref = refs[:6]
    pos = 6
    sink_ref = None
    if has_sink:
        sink_ref = refs[pos]
        pos += 1
    o_ref = refs[pos]
    lse_ref = refs[pos + 1] if want_lse else None

    first = pl.program_id(2) == 0
    prev_pen = jnp.where(first, -jnp.inf, 0.0).astype(F32)
    q = q_ref[0]
    kp, kc, vp, vc = kp_ref[0], kc_ref[0], vp_ref[0], vc_ref[0]
    outs, lses = [], []
    for h in range(n_heads):
        kv = h // grp
        qh = q[:, h * dh:(h + 1) * dh]
        sl = slice(kv * dh, (kv + 1) * dh)
        dn = (((1,), (1,)), ((), ()))
        bias = bias_ref[h]
        s_p = lax.dot_general(qh, kp[:, sl], dn, preferred_element_type=F32) * scale \
            + (bias[:, :ATT_BLOCK] + prev_pen)
        s_c = lax.dot_general(qh, kc[:, sl], dn, preferred_element_type=F32) * scale + bias[:, ATT_BLOCK:]
        m = jnp.maximum(jnp.max(s_p, axis=-1, keepdims=True), jnp.max(s_c, axis=-1, keepdims=True))
        if has_sink:
            sk = sink_ref[h][:, 0:1]
            m = jnp.maximum(m, sk)
        p_p = jnp.exp(s_p - m)
        p_c = jnp.exp(s_c - m)
        denom = jnp.sum(p_p, axis=-1, keepdims=True) + jnp.sum(p_c, axis=-1, keepdims=True)
        if has_sink:
            denom = denom + jnp.exp(sk - m)
        acc = jnp.dot(p_p.astype(BF16), vp[:, sl], preferred_element_type=F32) \
            + jnp.dot(p_c.astype(BF16), vc[:, sl], preferred_element_type=F32)
        outs.append(acc / denom)
        if want_lse:
            lses.append(m + jnp.log(denom))
    o_ref[0] = jnp.concatenate(outs, axis=-1).astype(o_ref.dtype)
    if want_lse:
        pad = jnp.zeros((ATT_BLOCK, LANES - n_heads), F32)
        lse_ref[0] = jnp.concatenate(lses + [pad], axis=-1)


def _band_bias(slopes, dist_scale, max_dist):
    rel = (jnp.arange(ATT_BLOCK)[:, None] + ATT_BLOCK) - jnp.arange(2 * ATT_BLOCK)[None, :]
    valid = (rel >= 0) & (rel <= max_dist)
    bias = -(slopes.astype(F32) * dist_scale)[:, None, None] * rel.astype(F32)[None]
    return jnp.where(valid[None], bias, -jnp.inf)


def banded_attention(qkv, batch, seq, dilation, *, n_heads, grp, dh, q_off, k_off, v_off, width,
                     bias, sinks=None, want_lse=False):
    d = dilation
    ld = seq // d
    nb = ld // ATT_BLOCK
    qw, kw = n_heads * dh, (n_heads // grp) * dh
    view = qkv.reshape(batch, ld, d * width)
    qb, kb, vb = q_off // qw, k_off // kw, v_off // kw
    assert q_off % qw == 0 and k_off % kw == 0 and v_off % kw == 0
    assert d == 1 or (width % qw == 0 and width % kw == 0)
    wb, wkb = width // qw, width // kw
    in_specs = [
        pl.BlockSpec((1, ATT_BLOCK, qw), lambda b, r, i: (b, i, r * wb + qb)),
        pl.BlockSpec((1, ATT_BLOCK, kw), lambda b, r, i: (b, jnp.maximum(i - 1, 0), r * wkb + kb)),
        pl.BlockSpec((1, ATT_BLOCK, kw), lambda b, r, i: (b, i, r * wkb + kb)),
        pl.BlockSpec((1, ATT_BLOCK, kw), lambda b, r, i: (b, jnp.maximum(i - 1, 0), r * wkb + vb)),
        pl.BlockSpec((1, ATT_BLOCK, kw), lambda b, r, i: (b, i, r * wkb + vb)),
        pl.BlockSpec((n_heads, ATT_BLOCK, 2 * ATT_BLOCK), lambda b, r, i: (0, 0, 0)),
    ]
    args = [view, view, view, view, view, bias]
    if sinks is not None:
        in_specs.append(pl.BlockSpec((n_heads, 1, LANES), lambda b, r, i: (0, 0, 0)))
        args.append(jnp.broadcast_to(sinks.astype(F32)[:, None, None], (n_heads, 1, LANES)))
    out_specs = [pl.BlockSpec((1, ATT_BLOCK, qw), lambda b, r, i: (b, i, r))]
    out_shape = [jax.ShapeDtypeStruct((batch, ld, d * qw), BF16)]
    if want_lse:
        out_specs.append(pl.BlockSpec((1, ATT_BLOCK, LANES), lambda b, r, i: (b, i, r)))
        out_shape.append(jax.ShapeDtypeStruct((batch, ld, d * LANES), F32))
    res = pl.pallas_call(
        functools.partial(_attn_kernel, n_heads=n_heads, grp=grp, dh=dh, scale=dh ** -0.5,
                          has_sink=sinks is not None, want_lse=want_lse),
        grid=(batch, d, nb),
        in_specs=in_specs, out_specs=out_specs, out_shape=out_shape,
        compiler_params=_cparams(("arbitrary", "arbitrary", "arbitrary"), 32),
        name=f"banded_attention_d{d}",
    )(*args)
    o = res[0].reshape(batch * seq, qw)
    if want_lse:
        return o, res[1].reshape(batch * seq, LANES)
    return o


def _merge_kernel(o1_ref, o2_ref, o3_ref, l1_ref, l2_ref, l3_ref, e_ref, out_ref):
    e = e_ref[...]
    ls = [jnp.dot(l[...], e, precision=HIGHEST, preferred_element_type=F32) for l in (l1_ref, l2_ref, l3_ref)]
    m = jnp.maximum(jnp.maximum(ls[0], ls[1]), ls[2])
    ws = [jnp.exp(l - m) for l in ls]
    num = ws[0] * o1_ref[...].astype(F32) + ws[1] * o2_ref[...].astype(F32) + ws[2] * o3_ref[...].astype(F32)
    out_ref[...] = (num / (ws[0] + ws[1] + ws[2])).astype(out_ref.dtype)


def merge_dilated(outs, lses):
    T, W = outs[0].shape
    tm = 512
    expand = (jnp.arange(LANES)[:, None] == (jnp.arange(W)[None, :] // DH_A)).astype(F32)
    return pl.pallas_call(
        _merge_kernel,
        grid=(T // tm,),
        in_specs=[pl.BlockSpec((tm, W), lambda i: (i, 0))] * 3
        + [pl.BlockSpec((tm, LANES), lambda i: (i, 0))] * 3
        + [pl.BlockSpec((LANES, W), lambda i: (0, 0))],
        out_specs=pl.BlockSpec((tm, W), lambda i: (i, 0)),
        out_shape=jax.ShapeDtypeStruct((T, W), BF16),
        compiler_params=_cparams(("arbitrary",), 32),
        name="merge_dilated",
    )(*outs, *lses, expand)


def _mm(a, b):
    return jnp.dot(a.astype(BF16), b.astype(BF16), preferred_element_type=F32)


def _dn_kernel(pb_ref, cw_ref, gpar_ref, ng_ref, o_ref, xbuf, state, qs, ks, vs, gbs, obuf, *, tb):
    C = DN_CHUNK
    i = pl.program_id(1)

    @pl.when(i == 0)
    def _():
        xbuf[0:SUBLANES, :] = jnp.zeros((SUBLANES, CONV_CH), F32)
        state[...] = jnp.zeros_like(state)

    xbuf[SUBLANES:SUBLANES + tb, :] = pb_ref[0, :, 0:CONV_CH]
    y = cw_ref[CONV_K - 1:CONV_K, :] * xbuf[SUBLANES:SUBLANES + tb, :]
    for j in range(CONV_K - 1):
        off = SUBLANES - (CONV_K - 1) + j
        y = y + cw_ref[j:j + 1, :] * xbuf[off:off + tb, :]
    xbuf[0:SUBLANES, :] = xbuf[tb:tb + SUBLANES, :]
    qkv = _silu(y)

    def l2n(t):
        return t * lax.rsqrt(jnp.sum(t * t, axis=-1, keepdims=True) + EPS)

    for kh in range(H_K_B):
        sl = slice(kh * DK_B, (kh + 1) * DK_B)
        qs[:, sl] = l2n(qkv[:, sl]) * (DK_B ** -0.5)
        ks[:, sl] = l2n(qkv[:, BK_W + kh * DK_B:BK_W + (kh + 1) * DK_B])
    vs[...] = qkv[:, 2 * BK_W:]

    gl = pb_ref[0, :, PB_GATE_OFF:PB_GATE_OFF + LANES]
    lane = lax.broadcasted_iota(jnp.int32, (tb, LANES), 1)
    beta = jax.nn.sigmoid(gl)
    gdec = -jnp.exp(gpar_ref[0:1, :]) * jax.nn.softplus(gl + gpar_ref[1:2, :])
    gbs[...] = jnp.where(lane < H_V_B, beta, jnp.where(lane < 2 * H_V_B, gdec, 0.0))

    row = lax.broadcasted_iota(jnp.int32, (C, C), 0)
    col = lax.broadcasted_iota(jnp.int32, (C, C), 1)
    incl = row >= col
    strict = row > col
    tri = incl.astype(F32)
    eye = (row == col).astype(F32)

    def chunk_body(c, carry):
        r0 = pl.multiple_of(c * C, C)
        gb_c = gbs[pl.ds(r0, C), :]
        gc_c = jnp.dot(tri, gb_c, precision=HIGHEST, preferred_element_type=F32)
        gc_t = gc_c.T
        for kh in range(H_K_B):
            sl = slice(kh * DK_B, (kh + 1) * DK_B)
            k_c = ks[pl.ds(r0, C), sl]
            q_c = qs[pl.ds(r0, C), sl]
            k_t = k_c.T
            gq = _mm(jnp.concatenate([k_c, q_c], axis=0), k_t)
            gram, qk_raw = gq[:C], gq[C:]
            for hv in range(kh * (H_V_B // H_K_B), (kh + 1) * (H_V_B // H_K_B)):
                beta_col = gb_c[:, hv:hv + 1]
                gcol = gc_c[:, H_V_B + hv:H_V_B + hv + 1]
                grow = gc_t[H_V_B + hv:H_V_B + hv + 1, :]
                decay = jnp.exp(jnp.where(incl, gcol - grow, -jnp.inf))
                lm = jnp.where(strict, gram * beta_col * decay, 0.0)
                tinv = eye
                for lvl in range(int(math.log2(C))):
                    rb = lax.shift_right_logical(row, lvl)
                    cb = lax.shift_right_logical(col, lvl)
                    l_off = jnp.where((rb == cb + 1) & ((rb & 1) == 1), lm, 0.0)
                    tinv = tinv - (l_off if lvl == 0 else _mm(_mm(tinv, l_off), tinv))
                egc = jnp.exp(gcol)
                v_c = vs[pl.ds(r0, C), hv * DV_B:(hv + 1) * DV_B]
                rhs = jnp.concatenate([v_c * beta_col, k_c * (beta_col * egc)], axis=1)
                sol = _mm(tinv, rhs)
                u, w = sol[:, :DV_B], sol[:, DV_B:]
                qk = jnp.where(incl, qk_raw * decay, 0.0)
                g_last = grow[:, C - 1:C]
                kd_t = k_t * jnp.exp(g_last - grow)
                s_h = state[hv]
                ws = _mm(jnp.concatenate([w, q_c * egc], axis=0), s_h)
                v_new = u - ws[:C]
                obuf[pl.ds(r0, C), hv * DV_B:(hv + 1) * DV_B] = ws[C:] + _mm(qk, v_new)
                state[hv] = s_h * jnp.exp(g_last) + _mm(kd_t, v_new)
        return carry

    lax.fori_loop(0, tb // C, chunk_body, 0)

    z = pb_ref[0, :, CONV_CH:CONV_CH + BV_W]
    ng = ng_ref[...]
    for hv in range(H_V_B):
        sl = slice(hv * DV_B, (hv + 1) * DV_B)
        o = obuf[:, sl]
        on = o * lax.rsqrt(jnp.mean(o * o, axis=-1, keepdims=True) + EPS) * ng
        o_ref[0, :, sl] = (on * _silu(z[:, sl])).astype(o_ref.dtype)


def gated_deltanet(pb, conv_w, a_log, dt_bias, norm_g, batch, seq):
    tb = 256
    view = pb.reshape(batch, seq, PB_W)
    gpar = jnp.zeros((2, LANES), F32)
    gpar = gpar.at[0, H_V_B:2 * H_V_B].set(a_log.astype(F32)).at[1, H_V_B:2 * H_V_B].set(dt_bias.astype(F32))
    out = pl.pallas_call(
        functools.partial(_dn_kernel, tb=tb),
        grid=(batch, seq // tb),
        in_specs=[pl.BlockSpec((1, tb, PB_W), lambda b, i: (b, i, 0)),
                  pl.BlockSpec((CONV_K, CONV_CH), lambda b, i: (0, 0)),
                  pl.BlockSpec((2, LANES), lambda b, i: (0, 0)),
                  pl.BlockSpec((1, DV_B), lambda b, i: (0, 0))],
        out_specs=pl.BlockSpec((1, tb, BV_W), lambda b, i: (b, i, 0)),
        out_shape=jax.ShapeDtypeStruct((batch, seq, BV_W), BF16),
        scratch_shapes=[pltpu.VMEM((tb + SUBLANES, CONV_CH), F32),
                        pltpu.VMEM((H_V_B, DK_B, DV_B), F32),
                        pltpu.VMEM((tb, BK_W), F32),
                        pltpu.VMEM((tb, BK_W), F32),
                        pltpu.VMEM((tb, BV_W), F32),
                        pltpu.VMEM((tb, LANES), F32),
                        pltpu.VMEM((tb, BV_W), F32)],
        compiler_params=_cparams(("arbitrary", "arbitrary"), 48),
        name="gated_deltanet",
    )(view, conv_w, gpar, norm_g.reshape(1, DV_B).astype(F32))
    return out.reshape(batch * seq, BV_W)


ROUTE_E1, ROUTE_E2, ROUTE_G1, ROUTE_G2 = 0, 1, 2, 3


def _pack_halves(h):
    half = h.shape[1] // 2
    lo = lax.bitcast_convert_type(h[:, :half].astype(BF16).astype(F32), jnp.uint32)
    hi = lax.bitcast_convert_type(h[:, half:].astype(BF16).astype(F32), jnp.uint32)
    return (hi & jnp.uint32(0xFFFF0000)) | (lo >> 16)


def _unpack_halves(w):
    lo = lax.bitcast_convert_type(w << 16, F32).astype(BF16)
    hi = lax.bitcast_convert_type(w & jnp.uint32(0xFFFF0000), F32).astype(BF16)
    return lo, hi


def _outproj_kernel(oa_ref, ob_ref, oc_ref, w_ref, x_ref, g1_ref, n2_ref, sc2_ref, sh2_ref, rw_ref, rb_ref,
                    x1_ref, hp_ref, route_ref, cnt_ref):
    i = pl.program_id(0)
    o = jnp.concatenate([oa_ref[...], ob_ref[...], oc_ref[...]], axis=-1)
    x1 = x_ref[...] + g1_ref[0] * jnp.dot(o, w_ref[...], preferred_element_type=F32)
    x1_ref[...] = x1
    h = x1 * lax.rsqrt(jnp.mean(x1 * x1, axis=-1, keepdims=True) + EPS) * n2_ref[...]
    h = h * (1.0 + sc2_ref[0]) + sh2_ref[0]
    hp_ref[...] = _pack_halves(h)

    logits = jnp.dot(h, rw_ref[...], precision=HIGHEST, preferred_element_type=F32) + rb_ref[...]
    tm = logits.shape[0]
    lane = lax.broadcasted_iota(jnp.int32, (tm, LANES), 1).astype(F32)
    big = float(LANES)
    glane = (lane >= N_EXPERTS) & (lane < N_EXPERTS + N_GROUPS)
    gmax = jnp.max(jnp.where(glane, logits, -jnp.inf), axis=-1, keepdims=True)
    gsel = jnp.min(jnp.where(glane & (logits == gmax), lane, big), axis=-1, keepdims=True) - N_EXPERTS
    pg = 1.0 / jnp.sum(jnp.where(glane, jnp.exp(logits - gmax), 0.0), axis=-1, keepdims=True)
    emask = (lane >= gsel * EXPERTS_PER_GROUP) & (lane < (gsel + 1.0) * EXPERTS_PER_GROUP)
    v1 = jnp.max(jnp.where(emask, logits, -jnp.inf), axis=-1, keepdims=True)
    i1 = jnp.min(jnp.where(emask & (logits == v1), lane, big), axis=-1, keepdims=True)
    m2 = emask & (lane != i1)
    v2 = jnp.max(jnp.where(m2, logits, -jnp.inf), axis=-1, keepdims=True)
    i2 = jnp.min(jnp.where(m2 & (logits == v2), lane, big), axis=-1, keepdims=True)
    t = jnp.exp(v2 - v1)
    p1 = 1.0 / (1.0 + t)
    p2 = t / (1.0 + t)
    route_ref[...] = jnp.where(lane == ROUTE_E1, i1, jnp.where(lane == ROUTE_E2, i2, jnp.where(
        lane == ROUTE_G1, pg * p1, jnp.where(lane == ROUTE_G2, pg * p2, 0.0))))

    @pl.when(i == 0)
    def _():
        cnt_ref[...] = jnp.zeros_like(cnt_ref)

    onehot = ((lane == i1) | (lane == i2)).astype(F32)
    cnt_ref[...] += jnp.broadcast_to(jnp.sum(onehot, axis=0, keepdims=True), cnt_ref.shape)


def output_projection(oa, ob, oc, w_out, x, mod, norm2_g, rw, rb, seq):
    T, D = x.shape
    tm = 256
    per_b = seq // tm
    modspec = lambda col: pl.BlockSpec((1, 1, D), lambda i: (i // per_b, 0, col))
    return pl.pallas_call(
        _outproj_kernel,
        grid=(T // tm,),
        in_specs=[pl.BlockSpec((tm, A_W), lambda i: (i, 0)),
                  pl.BlockSpec((tm, BV_W), lambda i: (i, 0)),
                  pl.BlockSpec((tm, C_Q_W), lambda i: (i, 0)),
                  pl.BlockSpec((D, D), lambda i: (0, 0)),
                  pl.BlockSpec((tm, D), lambda i: (i, 0)),
                  modspec(2),
                  pl.BlockSpec((1, D), lambda i: (0, 0)),
                  modspec(4), modspec(3),
                  pl.BlockSpec((D, LANES), lambda i: (0, 0)),
                  pl.BlockSpec((1, LANES), lambda i: (0, 0))],
        out_specs=[pl.BlockSpec((tm, D), lambda i: (i, 0)),
                   pl.BlockSpec((tm, D // 2), lambda i: (i, 0)),
                   pl.BlockSpec((tm, LANES), lambda i: (i, 0)),
                   pl.BlockSpec((SUBLANES, LANES), lambda i: (0, 0))],
        out_shape=[jax.ShapeDtypeStruct((T, D), F32),
                   jax.ShapeDtypeStruct((T, D // 2), jnp.uint32),
                   jax.ShapeDtypeStruct((T, LANES), F32),
                   jax.ShapeDtypeStruct((SUBLANES, LANES), F32)],
        compiler_params=_cparams(("arbitrary",), 48),
        name="output_projection_router",
    )(oa, ob, oc, w_out, x, mod, norm2_g.reshape(1, D), mod, mod, rw, rb)


EXPERT_TILE = 256


def _plan_kernel(route_ref, cnt_ref, pos_ref, be_ref, carry, pstart, *, tt, nblk_lanes):
    i = pl.program_id(0)
    r128 = lax.broadcasted_iota(jnp.int32, (LANES, LANES), 0)
    c128 = lax.broadcasted_iota(jnp.int32, (LANES, LANES), 1)

    @pl.when(i == 0)
    def _():
        cnt = cnt_ref[...]
        padded = jnp.ceil(cnt / EXPERT_TILE) * EXPERT_TILE
        upper = (r128 <= c128).astype(F32)
        pend = jnp.dot(padded, upper, precision=HIGHEST, preferred_element_type=F32)
        pstart[...] = pend - padded
        carry[...] = jnp.zeros_like(carry)
        pend_col = jnp.broadcast_to(pend[0:1, :], (LANES, LANES)).T
        ecol = r128 < N_EXPERTS
        for v in range(nblk_lanes // LANES):
            jrow = ((c128 + v * LANES) * EXPERT_TILE).astype(F32)
            n = jnp.sum(jnp.where(ecol & (pend_col <= jrow), 1.0, 0.0), axis=0, keepdims=True)
            be = jnp.minimum(n, float(N_EXPERTS - 1)).astype(jnp.int32)
            be_ref[:, v * LANES:(v + 1) * LANES] = jnp.broadcast_to(be, (SUBLANES, LANES))

    route = route_ref[...]
    lane = lax.broadcasted_iota(jnp.int32, (tt, LANES), 1).astype(F32)
    oh1 = lane == route[:, ROUTE_E1:ROUTE_E1 + 1]
    oh2 = lane == route[:, ROUTE_E2:ROUTE_E2 + 1]
    both = oh1.astype(F32) + oh2.astype(F32)
    rr = lax.broadcasted_iota(jnp.int32, (tt, tt), 0)
    cc = lax.broadcasted_iota(jnp.int32, (tt, tt), 1)
    before = jnp.dot((rr > cc).astype(BF16), both.astype(BF16), preferred_element_type=F32) + carry[0:1, :]
    base = before + pstart[0:1, :]
    p1 = jnp.sum(jnp.where(oh1, base, 0.0), axis=-1, keepdims=True)
    p2 = jnp.sum(jnp.where(oh2, base, 0.0), axis=-1, keepdims=True)
    pos_ref[...] = jnp.where(lane == 0.0, p1, jnp.where(lane == 1.0, p2, 0.0)).astype(jnp.int32)
    carry[...] += jnp.broadcast_to(jnp.sum(both, axis=0, keepdims=True), carry.shape)


def routing_plan(route, cnt, n_blocks):
    T = route.shape[0]
    tt = 512
    nblk_lanes = -(-n_blocks // LANES) * LANES
    return pl.pallas_call(
        functools.partial(_plan_kernel, tt=tt, nblk_lanes=nblk_lanes),
        grid=(T // tt,),
        in_specs=[pl.BlockSpec((tt, LANES), lambda i: (i, 0)),
                  pl.BlockSpec((SUBLANES, LANES), lambda i: (0, 0))],
        out_specs=[pl.BlockSpec((tt, LANES), lambda i: (i, 0)),
                   pl.BlockSpec((SUBLANES, nblk_lanes), lambda i: (0, 0))],
        out_shape=[jax.ShapeDtypeStruct((T, LANES), jnp.int32),
                   jax.ShapeDtypeStruct((SUBLANES, nblk_lanes), jnp.int32)],
        scratch_shapes=[pltpu.VMEM((SUBLANES, LANES), F32), pltpu.VMEM((SUBLANES, LANES), F32)],
        compiler_params=_cparams(("arbitrary",), 32),
        name="routing_plan",
    )(route, cnt)


def _row_copy(src_ref, dst_ref, src_row, dst_row, sem):
    return pltpu.make_async_copy(src_ref.at[pl.ds(src_row, 1)], dst_ref.at[pl.ds(dst_row, 1)], sem)


def _dispatch_kernel(pos_ref, hp_ref, xs_in_ref, xs_ref, sem, *, td):
    del xs_in_ref
    base = pl.program_id(0) * td

    def issue(r, c):
        for k in range(TOP_K):
            _row_copy(hp_ref, xs_ref, r, pos_ref[(base + r) * TOP_K + k], sem).start()
        return c

    lax.fori_loop(0, td, issue, 0, unroll=8)

    def drain(r, c):
        for k in range(TOP_K):
            _row_copy(hp_ref, xs_ref, 0, 0, sem).wait()
        return c

    lax.fori_loop(0, td, drain, 0, unroll=8)


def dispatch_rows(pos_flat, hp, n_rows):
    T, W = hp.shape
    td = 256
    return pl.pallas_call(
        functools.partial(_dispatch_kernel, td=td),
        grid_spec=pltpu.PrefetchScalarGridSpec(
            num_scalar_prefetch=1, grid=(T // td,),
            in_specs=[pl.BlockSpec((td, W), lambda i, pos: (i, 0)),
                      pl.BlockSpec(memory_space=pl.ANY)],
            out_specs=pl.BlockSpec(memory_space=pl.ANY),
            scratch_shapes=[pltpu.SemaphoreType.DMA(())]),
        out_shape=jax.ShapeDtypeStruct((n_rows, W), hp.dtype),
        input_output_aliases={2: 0},
        compiler_params=_cparams(("arbitrary",), 32),
        name="moe_dispatch",
    )(pos_flat, hp, jnp.zeros((n_rows, W), hp.dtype))


def _expert_kernel(be_ref, xs_ref, wg_ref, wu_ref, wd_ref, y_ref, wgb, wub, wdb):
    j = pl.program_id(0)
    changed = (j == 0) | (be_ref[j] != be_ref[jnp.maximum(j - 1, 0)])

    @pl.when(changed)
    def _():
        wgb[...] = wg_ref[0].astype(BF16)
        wub[...] = wu_ref[0].astype(BF16)
        wdb[...] = wd_ref[0].astype(BF16)

    lo, hi = _unpack_halves(xs_ref[...])
    half = lo.shape[1]
    gate = jnp.dot(lo, wgb[0:half, :], preferred_element_type=F32) \
        + jnp.dot(hi, wgb[half:, :], preferred_element_type=F32)
    up = jnp.dot(lo, wub[0:half, :], preferred_element_type=F32) \
        + jnp.dot(hi, wub[half:, :], preferred_element_type=F32)
    hid = (_silu(gate) * up).astype(BF16)
    y_ref[...] = jnp.dot(hid, wdb[...], preferred_element_type=F32)


def expert_mlp(block_expert, xs, w_gate, w_up, w_down):
    n_rows, W = xs.shape
    E, D, F = w_gate.shape
    n_blocks = n_rows // EXPERT_TILE
    return pl.pallas_call(
        _expert_kernel,
        grid_spec=pltpu.PrefetchScalarGridSpec(
            num_scalar_prefetch=1, grid=(n_blocks,),
            in_specs=[pl.BlockSpec((EXPERT_TILE, W), lambda j, be: (j, 0)),
                      pl.BlockSpec((1, D, F), lambda j, be: (be[j], 0, 0)),
                      pl.BlockSpec((1, D, F), lambda j, be: (be[j], 0, 0)),
                      pl.BlockSpec((1, F, D), lambda j, be: (be[j], 0, 0))],
            out_specs=pl.BlockSpec((EXPERT_TILE, D), lambda j, be: (j, 0)),
            scratch_shapes=[pltpu.VMEM((D, F), BF16), pltpu.VMEM((D, F), BF16), pltpu.VMEM((F, D), BF16)]),
        out_shape=jax.ShapeDtypeStruct((n_rows, D), F32),
        compiler_params=_cparams(("arbitrary",), 56),
        name="expert_mlp",
    )(block_expert, xs, w_gate, w_up, w_down)


def _combine_kernel(pos_ref, y_ref, x1_ref, g2_ref, route_ref, fg_ref, out_ref, ybuf, sem, *, tc, final_norm):
    base = pl.program_id(0) * tc

    def issue(r, c):
        for k in range(TOP_K):
            _row_copy(y_ref, ybuf.at[k], pos_ref[(base + r) * TOP_K + k], r, sem).start()
        return c

    lax.fori_loop(0, tc, issue, 0, unroll=8)

    def drain(r, c):
        for k in range(TOP_K):
            _row_copy(y_ref, ybuf.at[k], 0, 0, sem).wait()
        return c

    lax.fori_loop(0, tc, drain, 0, unroll=8)

    route = route_ref[...]
    moe = route[:, ROUTE_G1:ROUTE_G1 + 1] * ybuf[0] + route[:, ROUTE_G2:ROUTE_G2 + 1] * ybuf[1]
    x2 = x1_ref[...] + g2_ref[0] * moe
    if final_norm:
        x2 = x2 * lax.rsqrt(jnp.mean(x2 * x2, axis=-1, keepdims=True) + EPS) * fg_ref[...]
    out_ref[...] = x2


def combine_rows(pos_flat, y, x1, mod, route, final_g, seq, final_norm):
    T, D = x1.shape
    tc = 256
    per_b = seq // tc
    return pl.pallas_call(
        functools.partial(_combine_kernel, tc=tc, final_norm=final_norm),
        grid_spec=pltpu.PrefetchScalarGridSpec(
            num_scalar_prefetch=1, grid=(T // tc,),
            in_specs=[pl.BlockSpec(memory_space=pl.ANY),
                      pl.BlockSpec((tc, D), lambda i, pos: (i, 0)),
                      pl.BlockSpec((1, 1, D), lambda i, pos: (i // per_b, 0, 5)),
                      pl.BlockSpec((tc, LANES), lambda i, pos: (i, 0)),
                      pl.BlockSpec((1, D), lambda i, pos: (0, 0))],
            out_specs=pl.BlockSpec((tc, D), lambda i, pos: (i, 0)),
            scratch_shapes=[pltpu.VMEM((TOP_K, tc, D), F32), pltpu.SemaphoreType.DMA(())]),
        out_shape=jax.ShapeDtypeStruct((T, D), F32),
        compiler_params=_cparams(("arbitrary",), 40),
        name="moe_combine",
    )(pos_flat, y, x1, mod, route, final_g.reshape(1, D))


def _alibi_slopes(n):
    return 2.0 ** (-8.0 * jnp.arange(1, n + 1, dtype=F32) / n)


def _permute_w_in(w_in):
    L, D, _ = w_in.shape
    b_end = PA_W + PB_GATE_OFF
    g_end = b_end + 2 * H_V_B
    pad = jnp.zeros((L, D, LANES - 2 * H_V_B), w_in.dtype)
    return jnp.concatenate([w_in[:, :, :g_end], pad, w_in[:, :, g_end:]], axis=-1).astype(BF16)


def kernel(x, c, norm1_g, norm2_g, ada_w, ada_b, w_in, dn_conv_w, dn_a_log, dn_dt_bias, dn_norm_g, attn_sinks,
           w_out, router_group_w, router_group_b, router_expert_w, router_expert_b, expert_w_gate, expert_w_up,
           expert_w_down, final_norm_g):
    B, S, D = x.shape
    L = ada_w.shape[0]
    T = B * S
    n_rows = T * TOP_K + N_EXPERTS * EXPERT_TILE
    n_blocks = n_rows // EXPERT_TILE

    mod_all = ada_modulation(c, ada_w, ada_b)[:, :B].reshape(L, B, 1, 6 * D)
    w_perm = _permute_w_in(w_in)
    w_out_b = w_out.astype(BF16)
    rw = jnp.concatenate([router_expert_w, router_group_w,
                          jnp.zeros((L, D, LANES - N_EXPERTS - N_GROUPS), F32)], axis=-1)
    rb = jnp.concatenate([router_expert_b, router_group_b,
                          jnp.zeros((L, LANES - N_EXPERTS - N_GROUPS), F32)], axis=-1).reshape(L, 1, LANES)
    slopes_a = _alibi_slopes(H_A)
    bias_a = [_band_bias(slopes_a, float(dil), wdw // dil) for (wdw, dil) in DILATED_PATTERNS]
    bias_c = _band_bias(_alibi_slopes(H_C), 1.0, SWA_WINDOW - 1)

    xt = x.reshape(T, D)
    for l in range(L):
        mod = mod_all[l]
        pa, pb, pc = input_projection(xt, norm1_g[l], mod, w_perm[l], S)
        outs, lses = [], []
        for (wdw, dil), bias in zip(DILATED_PATTERNS, bias_a):
            o, lse = banded_attention(pa, B, S, dil, n_heads=H_A, grp=1, dh=DH_A, q_off=0, k_off=A_W,
                                      v_off=2 * A_W, width=PA_W, bias=bias, want_lse=True)
            outs.append(o)
            lses.append(lse)
        o_a = merge_dilated(outs, lses)
        o_b = gated_deltanet(pb, dn_conv_w[l], dn_a_log[l], dn_dt_bias[l], dn_norm_g[l], B, S)
        o_c = banded_attention(pc, B, S, 1, n_heads=H_C, grp=H_C // HKV_C, dh=DH_C, q_off=0, k_off=C_Q_W,
                               v_off=C_Q_W + C_KV_W, width=PC_W, bias=bias_c, sinks=attn_sinks[l])
        x1, hp, route, cnt = output_projection(o_a, o_b, o_c, w_out_b[l], xt, mod, norm2_g[l], rw[l], rb[l], S)
        pos, be = routing_plan(route, cnt, n_blocks)
        pos_flat = pos[:, :TOP_K].reshape(T * TOP_K)
        xs = dispatch_rows(pos_flat, hp, n_rows)
        y = expert_mlp(be[0, :n_blocks], xs, expert_w_gate[l], expert_w_up[l], expert_w_down[l])
        xt = combine_rows(pos_flat, y, x1, mod, route, final_norm_g, S, final_norm=(l == L - 1))
    return xt.reshape(B, S, D)
```

```python
import functools
import math

import jax
import jax.numpy as jnp
from jax import lax
from jax.experimental import pallas as pl
from jax.experimental.pallas import tpu as pltpu

F32 = jnp.float32
BF16 = jnp.bfloat16
HIGHEST = lax.Precision.HIGHEST

LANES = 128
SUBLANES = 8
MIB = 1024 * 1024

EPS = 1e-6
D_MODEL = 2048
H_A, DH_A = 8, 64
DILATED_PATTERNS = ((128, 1), (512, 4), (2048, 16))
H_K_B, H_V_B, DK_B, DV_B, CONV_K = 4, 8, 128, 128, 4
H_C, HKV_C, DH_C, SWA_WINDOW = 8, 2, 64, 128
ATT_BLOCK = 128
ATT_TILE = 2 * ATT_BLOCK
A_W = H_A * DH_A
BK_W = H_K_B * DK_B
BV_W = H_V_B * DV_B
C_Q_W = H_C * DH_C
C_KV_W = HKV_C * DH_C
CONV_CH = 2 * BK_W + BV_W
PA_W = 3 * A_W
PB_GATE_OFF = CONV_CH + BV_W
PB_W = PB_GATE_OFF + LANES
PC_KV_W = 2 * C_KV_W
PC_W = C_Q_W + 2 * PC_KV_W
N_PERM = PA_W + PB_W + PC_W
N_GROUPS, EXPERTS_PER_GROUP, TOP_K, D_FF = 4, 8, 2, 512
N_EXPERTS = N_GROUPS * EXPERTS_PER_GROUP
DN_CHUNK = 128


def _cparams(semantics, vmem_mib):
    return pltpu.CompilerParams(dimension_semantics=semantics, vmem_limit_bytes=vmem_mib * MIB)


def _silu(x):
    return x * jax.nn.sigmoid(x)


def _ada_kernel(c_ref, w_ref, b_ref, o_ref):
    c = c_ref[...]
    ca = _silu(c).astype(BF16)
    o_ref[0] = jnp.dot(ca, w_ref[0].astype(BF16), preferred_element_type=F32) + b_ref[0]


def ada_modulation(c, ada_w, ada_b):
    L, D, N = ada_w.shape
    B = c.shape[0]
    cp = jnp.zeros((SUBLANES, D), F32).at[:B].set(c)
    tn = 1024
    return pl.pallas_call(
        _ada_kernel,
        grid=(L, N // tn),
        in_specs=[pl.BlockSpec((SUBLANES, D), lambda l, j: (0, 0)),
                  pl.BlockSpec((1, D, tn), lambda l, j: (l, 0, j)),
                  pl.BlockSpec((1, 1, tn), lambda l, j: (l, 0, j))],
        out_specs=pl.BlockSpec((1, SUBLANES, tn), lambda l, j: (l, 0, j)),
        out_shape=jax.ShapeDtypeStruct((L, SUBLANES, N), F32),
        compiler_params=_cparams(("arbitrary", "arbitrary"), 40),
        name="ada_modulation",
    )(cp, ada_w, ada_b.reshape(L, 1, N))


def _inproj_kernel(x_ref, g_ref, sc_ref, sh_ref, w_ref, oa_ref, ob_ref, oc_ref):
    x = x_ref[...]
    y = x * lax.rsqrt(jnp.mean(x * x, axis=-1, keepdims=True) + EPS) * g_ref[...]
    hb = (y * (1.0 + sc_ref[0]) + sh_ref[0]).astype(BF16)
    oa_ref[...] = jnp.dot(hb, w_ref[0, :, 0:PA_W], preferred_element_type=F32).astype(BF16)
    ob_ref[...] = jnp.dot(hb, w_ref[0, :, PA_W:PA_W + PB_W], preferred_element_type=F32)
    oc_ref[...] = jnp.dot(hb, w_ref[0, :, PA_W + PB_W:N_PERM], preferred_element_type=F32).astype(BF16)


def input_projection(x, norm_g, mod, w_perm, layer, seq):
    T, D = x.shape
    tm = 256
    per_b = seq // tm
    return pl.pallas_call(
        _inproj_kernel,
        grid=(T // tm,),
        in_specs=[pl.BlockSpec((tm, D), lambda i: (i, 0)),
                  pl.BlockSpec((1, D), lambda i: (0, 0)),
                  pl.BlockSpec((1, 1, D), lambda i: (i // per_b, 0, 1)),
                  pl.BlockSpec((1, 1, D), lambda i: (i // per_b, 0, 0)),
                  pl.BlockSpec((1, D, N_PERM), lambda i: (layer, 0, 0), pipeline_mode=pl.Buffered(1))],
        out_specs=[pl.BlockSpec((tm, PA_W), lambda i: (i, 0)),
                   pl.BlockSpec((tm, PB_W), lambda i: (i, 0)),
                   pl.BlockSpec((tm, PC_W), lambda i: (i, 0))],
        out_shape=[jax.ShapeDtypeStruct((T, PA_W), BF16),
                   jax.ShapeDtypeStruct((T, PB_W), F32),
                   jax.ShapeDtypeStruct((T, PC_W), BF16)],
        compiler_params=_cparams(("arbitrary",), 56),
        name="input_projection",
    )(x, norm_g.reshape(1, D), mod, mod, w_perm)


def _attn_kernel(*refs, n_pairs, pairs_per_kv, dh, scale, has_sink, want_lse):
    q_ref, kp_ref, kc_ref, vp_ref, vc_ref, bias_ref = refs[:6]
    pos = 6
    sink_ref = None
    if has_sink:
        sink_ref = refs[pos]
        pos += 1
    o_ref = refs[pos]
    lse_ref = refs[pos + 1] if want_lse else None
    B = ATT_BLOCK

    first = pl.program_id(2) == 0
    key_col = lax.broadcasted_iota(jnp.int32, (1, 2 * B), 1)
    first_pen = jnp.where((key_col < B) & first, -jnp.inf, 0.0).astype(F32)
    lane = lax.broadcasted_iota(jnp.int32, (B, LANES), 1)
    lo = lane < dh
    lo16 = lo.astype(F32).astype(BF16)
    hi16 = (lane >= dh).astype(F32).astype(BF16)

    units = [(sb, j) for sb in range(ATT_TILE // B) for j in range(n_pairs)]

    def kv_rows(ref_p, ref_c, sb, j):
        sl = slice((j // pairs_per_kv) * LANES, (j // pairs_per_kv + 1) * LANES)
        if sb == 0:
            return jnp.concatenate([ref_p[0, :, sl], ref_c[0, 0:B, sl]], axis=0)
        return ref_c[0, (sb - 1) * B:(sb + 1) * B, sl]

    qq, kk, vv = [], [], []
    for sb, j in units:
        q2 = q_ref[0, sb * B:(sb + 1) * B, j * LANES:(j + 1) * LANES] * scale
        qq.append(jnp.concatenate([q2 * lo16, q2 * hi16], axis=0))
        kk.append(kv_rows(kp_ref, kc_ref, sb, j))
        vv.append(kv_rows(vp_ref, vc_ref, sb, j))
    dn = (((1,), (1,)), ((), ()))
    s = [lax.dot_general(qq[u], kk[u], dn, preferred_element_type=F32) for u in range(len(units))]
    s = [s[u] + (bias_ref[j] + first_pen if sb == 0 else bias_ref[j]) for u, (sb, j) in enumerate(units)]
    m = [jnp.max(jnp.maximum(x[:, :B], x[:, B:]), axis=-1, keepdims=True) for x in s]
    sk = None
    if has_sink:
        sk = [jnp.concatenate([jnp.broadcast_to(sink_ref[2 * j][:, 0:1], (B, 1)),
                               jnp.broadcast_to(sink_ref[2 * j + 1][:, 0:1], (B, 1))], axis=0)
              for (_, j) in units]
        m = [jnp.maximum(m[u], sk[u]) for u in range(len(units))]
    p = [jnp.exp(s[u] - m[u]) for u in range(len(units))]
    den = [jnp.sum(x[:, :B] + x[:, B:], axis=-1, keepdims=True) for x in p]
    if has_sink:
        den = [den[u] + jnp.exp(sk[u] - m[u]) for u in range(len(units))]
    acc = [jnp.dot(p[u].astype(BF16), vv[u], preferred_element_type=F32) / den[u] for u in range(len(units))]
    for u, (sb, j) in enumerate(units):
        o_ref[0, sb * B:(sb + 1) * B, j * LANES:(j + 1) * LANES] = jnp.where(
            lo, acc[u][:B], acc[u][B:]).astype(o_ref.dtype)
    if want_lse:
        for sb in range(ATT_TILE // B):
            tile = jnp.zeros((B, LANES), F32)
            for u, (sb_u, j) in enumerate(units):
                if sb_u == sb:
                    lse = m[u] + jnp.log(den[u])
                    tile = jnp.where(lane == 2 * j, lse[:B], jnp.where(lane == 2 * j + 1, lse[B:], tile))
            lse_ref[0, sb * B:(sb + 1) * B, :] = tile


def _band_bias(slopes, dist_scale, max_dist):
    rel = (jnp.arange(ATT_BLOCK)[:, None] + ATT_BLOCK) - jnp.arange(2 * ATT_BLOCK)[None, :]
    valid = (rel >= 0) & (rel <= max_dist)
    bias = -(slopes.astype(F32) * dist_scale)[:, None, None] * rel.astype(F32)[None]
    bias = jnp.where(valid[None], bias, -jnp.inf)
    return bias.reshape(slopes.shape[0] // 2, 2 * ATT_BLOCK, 2 * ATT_BLOCK)


def banded_attention(qkv, batch, seq, dilation, *, n_pairs, pairs_per_kv, dh, q_off, k_off, v_off, width,
                     bias, sinks=None, want_lse=False):
    d = dilation
    ld = seq // d
    assert ld % ATT_TILE == 0 and 2 * dh == LANES
    qw, kw = n_pairs * LANES, (n_pairs // pairs_per_kv) * LANES
    view = qkv.reshape(batch, ld, d * width)
    qb, kb, vb = q_off // qw, k_off // kw, v_off // kw
    assert q_off % qw == 0 and k_off % kw == 0 and v_off % kw == 0
    assert d == 1 or (width % qw == 0 and width % kw == 0)
    wb, wkb = width // qw, width // kw
    prev = lambda i: jnp.maximum(2 * i - 1, 0)
    in_specs = [
        pl.BlockSpec((1, ATT_TILE, qw), lambda b, r, i: (b, i, r * wb + qb)),
        pl.BlockSpec((1, ATT_BLOCK, kw), lambda b, r, i: (b, prev(i), r * wkb + kb)),
        pl.BlockSpec((1, ATT_TILE, kw), lambda b, r, i: (b, i, r * wkb + kb)),
        pl.BlockSpec((1, ATT_BLOCK, kw), lambda b, r, i: (b, prev(i), r * wkb + vb)),
        pl.BlockSpec((1, ATT_TILE, kw), lambda b, r, i: (b, i, r * wkb + vb)),
        pl.BlockSpec((n_pairs, 2 * ATT_BLOCK, 2 * ATT_BLOCK), lambda b, r, i: (0, 0, 0)),
    ]
    args = [view, view, view, view, view, bias]
    if sinks is not None:
        in_specs.append(pl.BlockSpec((2 * n_pairs, 1, LANES), lambda b, r, i: (0, 0, 0)))
        args.append(jnp.broadcast_to(sinks.astype(F32)[:, None, None], (2 * n_pairs, 1, LANES)))
    out_specs = [pl.BlockSpec((1, ATT_TILE, qw), lambda b, r, i: (b, i, r))]
    out_shape = [jax.ShapeDtypeStruct((batch, ld, d * qw), BF16)]
    if want_lse:
        out_specs.append(pl.BlockSpec((1, ATT_TILE, LANES), lambda b, r, i: (b, i, r)))
        out_shape.append(jax.ShapeDtypeStruct((batch, ld, d * LANES), F32))
    res = pl.pallas_call(
        functools.partial(_attn_kernel, n_pairs=n_pairs, pairs_per_kv=pairs_per_kv, dh=dh, scale=dh ** -0.5,
                          has_sink=sinks is not None, want_lse=want_lse),
        grid=(batch, d, ld // ATT_TILE),
        in_specs=in_specs, out_specs=out_specs, out_shape=out_shape,
        compiler_params=_cparams(("arbitrary", "arbitrary", "arbitrary"), 40),
        name=f"banded_attention_d{d}",
    )(*args)
    o = res[0].reshape(batch * seq, qw)
    if want_lse:
        return o, res[1].reshape(batch * seq, LANES)
    return o


def _merge_kernel(o1_ref, o2_ref, o3_ref, l1_ref, l2_ref, l3_ref, e_ref, out_ref):
    e = e_ref[...]
    ls = [jnp.dot(l[...], e, precision=HIGHEST, preferred_element_type=F32) for l in (l1_ref, l2_ref, l3_ref)]
    m = jnp.maximum(jnp.maximum(ls[0], ls[1]), ls[2])
    ws = [jnp.exp(l - m) for l in ls]
    num = ws[0] * o1_ref[...].astype(F32) + ws[1] * o2_ref[...].astype(F32) + ws[2] * o3_ref[...].astype(F32)
    out_ref[...] = (num / (ws[0] + ws[1] + ws[2])).astype(out_ref.dtype)


def merge_dilated(outs, lses):
    T, W = outs[0].shape
    tm = 512
    expand = (jnp.arange(LANES)[:, None] == (jnp.arange(W)[None, :] // DH_A)).astype(F32)
    return pl.pallas_call(
        _merge_kernel,
        grid=(T // tm,),
        in_specs=[pl.BlockSpec((tm, W), lambda i: (i, 0))] * 3
        + [pl.BlockSpec((tm, LANES), lambda i: (i, 0))] * 3
        + [pl.BlockSpec((LANES, W), lambda i: (0, 0))],
        out_specs=pl.BlockSpec((tm, W), lambda i: (i, 0)),
        out_shape=jax.ShapeDtypeStruct((T, W), BF16),
        compiler_params=_cparams(("arbitrary",), 32),
        name="merge_dilated",
    )(*outs, *lses, expand)


def _mm(a, b):
    return jnp.dot(a.astype(BF16), b.astype(BF16), preferred_element_type=F32)


def _dn_kernel(pb_ref, cw_ref, gpar_ref, ng_ref, lvl_ref, o_ref, xbuf, state, qs, ks, vs, gbs, obuf, *, tb):
    C = DN_CHUNK
    i = pl.program_id(1)

    @pl.when(i == 0)
    def _():
        xbuf[0:SUBLANES, :] = jnp.zeros((SUBLANES, CONV_CH), F32)
        state[...] = jnp.zeros_like(state)

    xbuf[SUBLANES:SUBLANES + tb, :] = pb_ref[0, :, 0:CONV_CH]
    y = cw_ref[CONV_K - 1:CONV_K, :] * xbuf[SUBLANES:SUBLANES + tb, :]
    for j in range(CONV_K - 1):
        off = SUBLANES - (CONV_K - 1) + j
        y = y + cw_ref[j:j + 1, :] * xbuf[off:off + tb, :]
    xbuf[0:SUBLANES, :] = xbuf[tb:tb + SUBLANES, :]
    qkv = _silu(y)

    def l2n(t):
        return t * lax.rsqrt(jnp.sum(t * t, axis=-1, keepdims=True) + EPS)

    for kh in range(H_K_B):
        sl = slice(kh * DK_B, (kh + 1) * DK_B)
        qs[:, sl] = l2n(qkv[:, sl]) * (DK_B ** -0.5)
        ks[:, sl] = l2n(qkv[:, BK_W + kh * DK_B:BK_W + (kh + 1) * DK_B])
    vs[...] = qkv[:, 2 * BK_W:]

    gl = pb_ref[0, :, PB_GATE_OFF:PB_GATE_OFF + LANES]
    lane = lax.broadcasted_iota(jnp.int32, (tb, LANES), 1)
    beta = jax.nn.sigmoid(gl)
    gdec = -jnp.exp(gpar_ref[0:1, :]) * jax.nn.softplus(gl + gpar_ref[1:2, :])
    gbs[...] = jnp.where(lane < H_V_B, beta, jnp.where(lane < 2 * H_V_B, gdec, 0.0))

    row = lax.broadcasted_iota(jnp.int32, (C, C), 0)
    col = lax.broadcasted_iota(jnp.int32, (C, C), 1)
    incl = row >= col
    strict = row > col
    tri = incl.astype(F32)
    eye = (row == col).astype(F32)
    heads = range(H_V_B)
    rep = H_V_B // H_K_B

    def chunk_body(c, carry):
        r0 = pl.multiple_of(c * C, C)
        gb_c = gbs[pl.ds(r0, C), :]
        gc_c = jnp.dot(tri, gb_c, precision=HIGHEST, preferred_element_type=F32)
        gc_t = gc_c.T
        k_c = [ks[pl.ds(r0, C), kh * DK_B:(kh + 1) * DK_B] for kh in range(H_K_B)]
        q_c = [qs[pl.ds(r0, C), kh * DK_B:(kh + 1) * DK_B] for kh in range(H_K_B)]
        k_t = [k.T for k in k_c]
        gq = [_mm(jnp.concatenate([k_c[kh], q_c[kh]], axis=0), k_t[kh]) for kh in range(H_K_B)]
        beta_col = [gb_c[:, h:h + 1] for h in heads]
        gcol = [gc_c[:, H_V_B + h:H_V_B + h + 1] for h in heads]
        grow = [gc_t[H_V_B + h:H_V_B + h + 1, :] for h in heads]
        decay = [jnp.exp(jnp.where(incl, gcol[h] - grow[h], -jnp.inf)) for h in heads]
        lmb = [jnp.where(strict, gq[h // rep][:C] * beta_col[h] * decay[h], 0.0).astype(BF16) for h in heads]
        tinv = [eye - (lmb[h] * lvl_ref[0]).astype(F32) for h in heads]
        for lvl in range(1, int(math.log2(C))):
            t16 = [t.astype(BF16) for t in tinv]
            a16 = [jnp.dot(t16[h], lmb[h] * lvl_ref[lvl], preferred_element_type=F32).astype(BF16) for h in heads]
            tinv = [tinv[h] - jnp.dot(a16[h], t16[h], preferred_element_type=F32) for h in heads]
        egc = [jnp.exp(g) for g in gcol]
        v_c = [vs[pl.ds(r0, C), h * DV_B:(h + 1) * DV_B] for h in heads]
        rhs = [jnp.concatenate([v_c[h] * beta_col[h], k_c[h // rep] * (beta_col[h] * egc[h])], axis=1)
               for h in heads]
        sol = [_mm(tinv[h], rhs[h]) for h in heads]
        qk = [jnp.where(incl, gq[h // rep][C:] * decay[h], 0.0).astype(BF16) for h in heads]
        g_last = [g[:, C - 1:C] for g in grow]
        kd_t = [(k_t[h // rep] * jnp.exp(g_last[h] - grow[h])).astype(BF16) for h in heads]
        s_h = [state[h] for h in heads]
        ws = [_mm(jnp.concatenate([sol[h][:, DV_B:], q_c[h // rep] * egc[h]], axis=0), s_h[h]) for h in heads]
        v_new = [(sol[h][:, :DV_B] - ws[h][:C]).astype(BF16) for h in heads]
        for h in heads:
            obuf[pl.ds(r0, C), h * DV_B:(h + 1) * DV_B] = ws[h][C:] + jnp.dot(
                qk[h], v_new[h], preferred_element_type=F32)
        for h in heads:
            state[h] = s_h[h] * jnp.exp(g_last[h]) + jnp.dot(kd_t[h], v_new[h], preferred_element_type=F32)
        return carry

    lax.fori_loop(0, tb // C, chunk_body, 0)

    z = pb_ref[0, :, CONV_CH:CONV_CH + BV_W]
    ng = ng_ref[...]
    for hv in range(H_V_B):
        sl = slice(hv * DV_B, (hv + 1) * DV_B)
        o = obuf[:, sl]
        on = o * lax.rsqrt(jnp.mean(o * o, axis=-1, keepdims=True) + EPS) * ng
        o_ref[0, :, sl] = (on * _silu(z[:, sl])).astype(o_ref.dtype)


def gated_deltanet(pb, conv_w, a_log, dt_bias, norm_g, batch, seq):
    tb = 256
    view = pb.reshape(batch, seq, PB_W)
    gpar = jnp.zeros((2, LANES), F32)
    gpar = gpar.at[0, H_V_B:2 * H_V_B].set(a_log.astype(F32)).at[1, H_V_B:2 * H_V_B].set(dt_bias.astype(F32))
    n_lvl = int(math.log2(DN_CHUNK))
    ii = jnp.arange(DN_CHUNK)
    lvl_masks = jnp.stack([((ii[:, None] >> v) == (ii[None, :] >> v) + 1) & (((ii[:, None] >> v) & 1) == 1)
                           for v in range(n_lvl)]).astype(BF16)
    out = pl.pallas_call(
        functools.partial(_dn_kernel, tb=tb),
        grid=(batch, seq // tb),
        in_specs=[pl.BlockSpec((1, tb, PB_W), lambda b, i: (b, i, 0)),
                  pl.BlockSpec((CONV_K, CONV_CH), lambda b, i: (0, 0)),
                  pl.BlockSpec((2, LANES), lambda b, i: (0, 0)),
                  pl.BlockSpec((1, DV_B), lambda b, i: (0, 0)),
                  pl.BlockSpec((n_lvl, DN_CHUNK, DN_CHUNK), lambda b, i: (0, 0, 0))],
        out_specs=pl.BlockSpec((1, tb, BV_W), lambda b, i: (b, i, 0)),
        out_shape=jax.ShapeDtypeStruct((batch, seq, BV_W), BF16),
        scratch_shapes=[pltpu.VMEM((tb + SUBLANES, CONV_CH), F32),
                        pltpu.VMEM((H_V_B, DK_B, DV_B), F32),
                        pltpu.VMEM((tb, BK_W), F32),
                        pltpu.VMEM((tb, BK_W), F32),
                        pltpu.VMEM((tb, BV_W), F32),
                        pltpu.VMEM((tb, LANES), F32),
                        pltpu.VMEM((tb, BV_W), F32)],
        compiler_params=_cparams(("arbitrary", "arbitrary"), 48),
        name="gated_deltanet",
    )(view, conv_w, gpar, norm_g.reshape(1, DV_B).astype(F32), lvl_masks)
    return out.reshape(batch * seq, BV_W)


ROUTE_E1, ROUTE_E2, ROUTE_G1, ROUTE_G2 = 0, 1, 2, 3


def _pack_halves(hb):
    half = hb.shape[1] // 2
    lo = lax.bitcast_convert_type(hb[:, :half].astype(F32), jnp.uint32)
    hi = lax.bitcast_convert_type(hb[:, half:].astype(F32), jnp.uint32)
    return (hi & jnp.uint32(0xFFFF0000)) | (lo >> 16)


def _unpack_halves(w):
    lo = lax.bitcast_convert_type(w << 16, F32).astype(BF16)
    hi = lax.bitcast_convert_type(w & jnp.uint32(0xFFFF0000), F32).astype(BF16)
    return lo, hi


def _outproj_kernel(oa_ref, ob_ref, oc_ref, w_ref, x_ref, g1_ref, n2_ref, sc2_ref, sh2_ref, rw_ref, rb_ref,
                    x1_ref, hp_ref, route_ref, cnt_ref):
    i = pl.program_id(0)
    o = jnp.concatenate([oa_ref[...], ob_ref[...], oc_ref[...]], axis=-1)
    x1 = x_ref[...] + g1_ref[0] * jnp.dot(o, w_ref[0], preferred_element_type=F32)
    x1_ref[...] = x1
    h = x1 * lax.rsqrt(jnp.mean(x1 * x1, axis=-1, keepdims=True) + EPS) * n2_ref[...]
    h = h * (1.0 + sc2_ref[0]) + sh2_ref[0]
    h_hi = h.astype(BF16)
    hp_ref[...] = _pack_halves(h_hi)

    h_lo = (h - h_hi.astype(F32)).astype(BF16)
    hw = jnp.dot(h_hi, rw_ref[0], preferred_element_type=F32)
    logits = hw[:, :LANES] + hw[:, LANES:] + jnp.dot(h_lo, rw_ref[0, :, 0:LANES], preferred_element_type=F32) \
        + rb_ref[...]
    tm = logits.shape[0]
    lane = lax.broadcasted_iota(jnp.int32, (tm, LANES), 1).astype(F32)
    big = float(LANES)
    glane = (lane >= N_EXPERTS) & (lane < N_EXPERTS + N_GROUPS)
    gmax = jnp.max(jnp.where(glane, logits, -jnp.inf), axis=-1, keepdims=True)
    gsel = jnp.min(jnp.where(glane & (logits == gmax), lane, big), axis=-1, keepdims=True) - N_EXPERTS
    pg = 1.0 / jnp.sum(jnp.where(glane, jnp.exp(logits - gmax), 0.0), axis=-1, keepdims=True)
    emask = (lane >= gsel * EXPERTS_PER_GROUP) & (lane < (gsel + 1.0) * EXPERTS_PER_GROUP)
    v1 = jnp.max(jnp.where(emask, logits, -jnp.inf), axis=-1, keepdims=True)
    i1 = jnp.min(jnp.where(emask & (logits == v1), lane, big), axis=-1, keepdims=True)
    m2 = emask & (lane != i1)
    v2 = jnp.max(jnp.where(m2, logits, -jnp.inf), axis=-1, keepdims=True)
    i2 = jnp.min(jnp.where(m2 & (logits == v2), lane, big), axis=-1, keepdims=True)
    t = jnp.exp(v2 - v1)
    p1 = 1.0 / (1.0 + t)
    p2 = t / (1.0 + t)
    route_ref[...] = jnp.where(lane == ROUTE_E1, i1, jnp.where(lane == ROUTE_E2, i2, jnp.where(
        lane == ROUTE_G1, pg * p1, jnp.where(lane == ROUTE_G2, pg * p2, 0.0))))

    @pl.when(i == 0)
    def _():
        cnt_ref[...] = jnp.zeros_like(cnt_ref)

    onehot = ((lane == i1) | (lane == i2)).astype(F32)
    cnt_ref[...] += jnp.broadcast_to(jnp.sum(onehot, axis=0, keepdims=True), cnt_ref.shape)


def output_projection(oa, ob, oc, w_out, x, mod, norm2_g, rw2, rb, layer, seq):
    T, D = x.shape
    tm = 256
    per_b = seq // tm
    modspec = lambda col: pl.BlockSpec((1, 1, D), lambda i: (i // per_b, 0, col))
    return pl.pallas_call(
        _outproj_kernel,
        grid=(T // tm,),
        in_specs=[pl.BlockSpec((tm, A_W), lambda i: (i, 0)),
                  pl.BlockSpec((tm, BV_W), lambda i: (i, 0)),
                  pl.BlockSpec((tm, C_Q_W), lambda i: (i, 0)),
                  pl.BlockSpec((1, D, D), lambda i: (layer, 0, 0)),
                  pl.BlockSpec((tm, D), lambda i: (i, 0)),
                  modspec(2),
                  pl.BlockSpec((1, D), lambda i: (0, 0)),
                  modspec(4), modspec(3),
                  pl.BlockSpec((1, D, 2 * LANES), lambda i: (layer, 0, 0)),
                  pl.BlockSpec((1, LANES), lambda i: (0, 0))],
        out_specs=[pl.BlockSpec((tm, D), lambda i: (i, 0)),
                   pl.BlockSpec((tm, D // 2), lambda i: (i, 0)),
                   pl.BlockSpec((tm, LANES), lambda i: (i, 0)),
                   pl.BlockSpec((SUBLANES, LANES), lambda i: (0, 0))],
        out_shape=[jax.ShapeDtypeStruct((T, D), F32),
                   jax.ShapeDtypeStruct((T, D // 2), jnp.uint32),
                   jax.ShapeDtypeStruct((T, LANES), F32),
                   jax.ShapeDtypeStruct((SUBLANES, LANES), F32)],
        compiler_params=_cparams(("arbitrary",), 48),
        name="output_projection_router",
    )(oa, ob, oc, w_out, x, mod, norm2_g.reshape(1, D), mod, mod, rw2, rb)


EXPERT_TILE = 256


def _plan_kernel(route_ref, cnt_ref, pos_ref, be_ref, carry, pstart, *, tt, nblk_lanes):
    i = pl.program_id(0)
    r128 = lax.broadcasted_iota(jnp.int32, (LANES, LANES), 0)
    c128 = lax.broadcasted_iota(jnp.int32, (LANES, LANES), 1)

    @pl.when(i == 0)
    def _():
        cnt = cnt_ref[...]
        padded = jnp.ceil(cnt / EXPERT_TILE) * EXPERT_TILE
        upper = (r128 <= c128).astype(F32)
        pend = jnp.dot(padded, upper, precision=HIGHEST, preferred_element_type=F32)
        pstart[...] = pend - padded
        carry[...] = jnp.zeros_like(carry)
        pend_col = jnp.broadcast_to(pend[0:1, :], (LANES, LANES)).T
        ecol = r128 < N_EXPERTS
        for v in range(nblk_lanes // LANES):
            jrow = ((c128 + v * LANES) * EXPERT_TILE).astype(F32)
            n = jnp.sum(jnp.where(ecol & (pend_col <= jrow), 1.0, 0.0), axis=0, keepdims=True)
            be = jnp.minimum(n, float(N_EXPERTS - 1)).astype(jnp.int32)
            be_ref[:, v * LANES:(v + 1) * LANES] = jnp.broadcast_to(be, (SUBLANES, LANES))

    route = route_ref[...]
    lane = lax.broadcasted_iota(jnp.int32, (tt, LANES), 1).astype(F32)
    oh1 = lane == route[:, ROUTE_E1:ROUTE_E1 + 1]
    oh2 = lane == route[:, ROUTE_E2:ROUTE_E2 + 1]
    both = oh1.astype(F32) + oh2.astype(F32)
    rr = lax.broadcasted_iota(jnp.int32, (tt, tt), 0)
    cc = lax.broadcasted_iota(jnp.int32, (tt, tt), 1)
    before = jnp.dot((rr > cc).astype(BF16), both.astype(BF16), preferred_element_type=F32) + carry[0:1, :]
    base = before + pstart[0:1, :]
    p1 = jnp.sum(jnp.where(oh1, base, 0.0), axis=-1, keepdims=True)
    p2 = jnp.sum(jnp.where(oh2, base, 0.0), axis=-1, keepdims=True)
    pos_ref[...] = jnp.where(lane == 0.0, p1, jnp.where(lane == 1.0, p2, 0.0)).astype(jnp.int32)
    carry[...] += jnp.broadcast_to(jnp.sum(both, axis=0, keepdims=True), carry.shape)


def routing_plan(route, cnt, n_blocks):
    T = route.shape[0]
    tt = 512
    nblk_lanes = -(-n_blocks // LANES) * LANES
    return pl.pallas_call(
        functools.partial(_plan_kernel, tt=tt, nblk_lanes=nblk_lanes),
        grid=(T // tt,),
        in_specs=[pl.BlockSpec((tt, LANES), lambda i: (i, 0)),
                  pl.BlockSpec((SUBLANES, LANES), lambda i: (0, 0))],
        out_specs=[pl.BlockSpec((tt, LANES), lambda i: (i, 0)),
                   pl.BlockSpec((SUBLANES, nblk_lanes), lambda i: (0, 0))],
        out_shape=[jax.ShapeDtypeStruct((T, LANES), jnp.int32),
                   jax.ShapeDtypeStruct((SUBLANES, nblk_lanes), jnp.int32)],
        scratch_shapes=[pltpu.VMEM((SUBLANES, LANES), F32), pltpu.VMEM((SUBLANES, LANES), F32)],
        compiler_params=_cparams(("arbitrary",), 32),
        name="routing_plan",
    )(route, cnt)


def _row_copy(src_ref, dst_ref, src_row, dst_row, sem):
    return pltpu.make_async_copy(src_ref.at[pl.ds(src_row, 1)], dst_ref.at[pl.ds(dst_row, 1)], sem)


def _dispatch_kernel(pos_ref, hp_ref, xs_in_ref, xs_ref, sem, *, td):
    del xs_in_ref
    base = pl.program_id(0) * td

    def issue(r, c):
        for k in range(TOP_K):
            _row_copy(hp_ref, xs_ref, r, pos_ref[(base + r) * TOP_K + k], sem).start()
        return c

    lax.fori_loop(0, td, issue, 0, unroll=8)

    def drain(r, c):
        for k in range(TOP_K):
            _row_copy(hp_ref, xs_ref, 0, 0, sem).wait()
        return c

    lax.fori_loop(0, td, drain, 0, unroll=8)


def dispatch_rows(pos_flat, hp, n_rows):
    T, W = hp.shape
    td = 256
    return pl.pallas_call(
        functools.partial(_dispatch_kernel, td=td),
        grid_spec=pltpu.PrefetchScalarGridSpec(
            num_scalar_prefetch=1, grid=(T // td,),
            in_specs=[pl.BlockSpec((td, W), lambda i, pos: (i, 0)),
                      pl.BlockSpec(memory_space=pl.ANY)],
            out_specs=pl.BlockSpec(memory_space=pl.ANY),
            scratch_shapes=[pltpu.SemaphoreType.DMA(())]),
        out_shape=jax.ShapeDtypeStruct((n_rows, W), hp.dtype),
        input_output_aliases={2: 0},
        compiler_params=_cparams(("arbitrary",), 32),
        name="moe_dispatch",
    )(pos_flat, hp, jnp.zeros((n_rows, W), hp.dtype))


def _expert_kernel(be_ref, xs_ref, wg_ref, wu_ref, wd_ref, y_ref, wgb, wub, wdb):
    j = pl.program_id(0)
    changed = (j == 0) | (be_ref[j] != be_ref[jnp.maximum(j - 1, 0)])

    @pl.when(changed)
    def _():
        wgb[...] = wg_ref[0, 0].astype(BF16)
        wub[...] = wu_ref[0, 0].astype(BF16)
        wdb[...] = wd_ref[0, 0].astype(BF16)

    lo, hi = _unpack_halves(xs_ref[...])
    half = lo.shape[1]
    gate = jnp.dot(lo, wgb[0:half, :], preferred_element_type=F32) \
        + jnp.dot(hi, wgb[half:, :], preferred_element_type=F32)
    up = jnp.dot(lo, wub[0:half, :], preferred_element_type=F32) \
        + jnp.dot(hi, wub[half:, :], preferred_element_type=F32)
    hid = (_silu(gate) * up).astype(BF16)
    y_ref[...] = jnp.dot(hid, wdb[...], preferred_element_type=F32)


def expert_mlp(block_expert, xs, w_gate, w_up, w_down, layer):
    n_rows, W = xs.shape
    _, E, D, F = w_gate.shape
    n_blocks = n_rows // EXPERT_TILE
    return pl.pallas_call(
        _expert_kernel,
        grid_spec=pltpu.PrefetchScalarGridSpec(
            num_scalar_prefetch=1, grid=(n_blocks,),
            in_specs=[pl.BlockSpec((EXPERT_TILE, W), lambda j, be: (j, 0)),
                      pl.BlockSpec((1, 1, D, F), lambda j, be: (layer, be[j], 0, 0)),
                      pl.BlockSpec((1, 1, D, F), lambda j, be: (layer, be[j], 0, 0)),
                      pl.BlockSpec((1, 1, F, D), lambda j, be: (layer, be[j], 0, 0))],
            out_specs=pl.BlockSpec((EXPERT_TILE, D), lambda j, be: (j, 0)),
            scratch_shapes=[pltpu.VMEM((D, F), BF16), pltpu.VMEM((D, F), BF16), pltpu.VMEM((F, D), BF16)]),
        out_shape=jax.ShapeDtypeStruct((n_rows, D), F32),
        compiler_params=_cparams(("arbitrary",), 56),
        name="expert_mlp",
    )(block_expert, xs, w_gate, w_up, w_down)


def _combine_kernel(pos_ref, y_ref, x1_ref, g2_ref, route_ref, fg_ref, out_ref, ybuf, sem, *, tc, final_norm):
    base = pl.program_id(0) * tc

    def issue(r, c):
        for k in range(TOP_K):
            _row_copy(y_ref, ybuf.at[k], pos_ref[(base + r) * TOP_K + k], r, sem).start()
        return c

    lax.fori_loop(0, tc, issue, 0, unroll=8)

    def drain(r, c):
        for k in range(TOP_K):
            _row_copy(y_ref, ybuf.at[k], 0, 0, sem).wait()
        return c

    lax.fori_loop(0, tc, drain, 0, unroll=8)

    route = route_ref[...]
    moe = route[:, ROUTE_G1:ROUTE_G1 + 1] * ybuf[0] + route[:, ROUTE_G2:ROUTE_G2 + 1] * ybuf[1]
    x2 = x1_ref[...] + g2_ref[0] * moe
    if final_norm:
        x2 = x2 * lax.rsqrt(jnp.mean(x2 * x2, axis=-1, keepdims=True) + EPS) * fg_ref[...]
    out_ref[...] = x2


def combine_rows(pos_flat, y, x1, mod, route, final_g, seq, final_norm):
    T, D = x1.shape
    tc = 256
    per_b = seq // tc
    return pl.pallas_call(
        functools.partial(_combine_kernel, tc=tc, final_norm=final_norm),
        grid_spec=pltpu.PrefetchScalarGridSpec(
            num_scalar_prefetch=1, grid=(T // tc,),
            in_specs=[pl.BlockSpec(memory_space=pl.ANY),
                      pl.BlockSpec((tc, D), lambda i, pos: (i, 0)),
                      pl.BlockSpec((1, 1, D), lambda i, pos: (i // per_b, 0, 5)),
                      pl.BlockSpec((tc, LANES), lambda i, pos: (i, 0)),
                      pl.BlockSpec((1, D), lambda i, pos: (0, 0))],
            out_specs=pl.BlockSpec((tc, D), lambda i, pos: (i, 0)),
            scratch_shapes=[pltpu.VMEM((TOP_K, tc, D), F32), pltpu.SemaphoreType.DMA(())]),
        out_shape=jax.ShapeDtypeStruct((T, D), F32),
        compiler_params=_cparams(("arbitrary",), 40),
        name="moe_combine",
    )(pos_flat, y, x1, mod, route, final_g.reshape(1, D))


def _alibi_slopes(n):
    return 2.0 ** (-8.0 * jnp.arange(1, n + 1, dtype=F32) / n)


def _permute_w_in(w_in):
    L, D, _ = w_in.shape
    g_end = PA_W + PB_GATE_OFF + 2 * H_V_B
    pad = jnp.zeros((L, D, LANES - 2 * H_V_B), w_in.dtype)
    cq = w_in[:, :, g_end:g_end + C_Q_W]
    ck = w_in[:, :, g_end + C_Q_W:g_end + C_Q_W + C_KV_W].reshape(L, D, HKV_C, DH_C)
    cv = w_in[:, :, g_end + C_Q_W + C_KV_W:].reshape(L, D, HKV_C, DH_C)
    dup = lambda t: jnp.concatenate([t, t], axis=-1).reshape(L, D, PC_KV_W)
    return jnp.concatenate([w_in[:, :, :g_end], pad, cq, dup(ck), dup(cv)], axis=-1).astype(BF16)


def kernel(x, c, norm1_g, norm2_g, ada_w, ada_b, w_in, dn_conv_w, dn_a_log, dn_dt_bias, dn_norm_g, attn_sinks,
           w_out, router_group_w, router_group_b, router_expert_w, router_expert_b, expert_w_gate, expert_w_up,
           expert_w_down, final_norm_g):
    B, S, D = x.shape
    L = ada_w.shape[0]
    T = B * S
    n_rows = T * TOP_K + N_EXPERTS * EXPERT_TILE
    n_blocks = n_rows // EXPERT_TILE

    mod_all = ada_modulation(c, ada_w, ada_b)[:, :B].reshape(L, B, 1, 6 * D)
    w_perm = _permute_w_in(w_in)
    w_out_b = w_out.astype(BF16)
    rw = jnp.concatenate([router_expert_w, router_group_w,
                          jnp.zeros((L, D, LANES - N_EXPERTS - N_GROUPS), F32)], axis=-1)
    rw_hi = rw.astype(BF16)
    rw2 = jnp.concatenate([rw_hi, (rw - rw_hi.astype(F32)).astype(BF16)], axis=-1)
    rb = jnp.concatenate([router_expert_b, router_group_b,
                          jnp.zeros((L, LANES - N_EXPERTS - N_GROUPS), F32)], axis=-1).reshape(L, 1, LANES)
    slopes_a = _alibi_slopes(H_A)
    bias_a = [_band_bias(slopes_a, float(dil), wdw // dil) for (wdw, dil) in DILATED_PATTERNS]
    bias_c = _band_bias(_alibi_slopes(H_C), 1.0, SWA_WINDOW - 1)

    xt = x.reshape(T, D)
    for l in range(L):
        mod = mod_all[l]
        pa, pb, pc = input_projection(xt, norm1_g[l], mod, w_perm, l, S)
        outs, lses = [], []
        for (wdw, dil), bias in zip(DILATED_PATTERNS, bias_a):
            o, lse = banded_attention(pa, B, S, dil, n_pairs=H_A // 2, pairs_per_kv=1, dh=DH_A, q_off=0, k_off=A_W,
                                      v_off=2 * A_W, width=PA_W, bias=bias, want_lse=True)
            outs.append(o)
            lses.append(lse)
        o_a = merge_dilated(outs, lses)
        o_b = gated_deltanet(pb, dn_conv_w[l], dn_a_log[l], dn_dt_bias[l], dn_norm_g[l], B, S)
        o_c = banded_attention(pc, B, S, 1, n_pairs=H_C // 2, pairs_per_kv=H_C // HKV_C // 2, dh=DH_C, q_off=0,
                               k_off=C_Q_W, v_off=C_Q_W + PC_KV_W, width=PC_W, bias=bias_c, sinks=attn_sinks[l])
        x1, hp, route, cnt = output_projection(o_a, o_b, o_c, w_out_b, xt, mod, norm2_g[l], rw2, rb[l], l, S)
        pos, be = routing_plan(route, cnt, n_blocks)
        pos_flat = pos[:, :TOP_K].reshape(T * TOP_K)
        xs = dispatch_rows(pos_flat, hp, n_rows)
        y = expert_mlp(be[0, :n_blocks], xs, expert_w_gate, expert_w_up, expert_w_down, l)
        xt = combine_rows(pos_flat, y, x1, mod, route, final_norm_g, S, final_norm=(l == L - 1))
    return xt.reshape(B, S, D)
```

```python
import functools
import math

import jax
import jax.numpy as jnp
from jax import lax
from jax.experimental import pallas as pl
from jax.experimental.pallas import tpu as pltpu

F32 = jnp.float32
BF16 = jnp.bfloat16
HIGHEST = lax.Precision.HIGHEST

LANES = 128
SUBLANES = 8
MIB = 1024 * 1024

EPS = 1e-6
D_MODEL = 2048
H_A, DH_A = 8, 64
DILATED_PATTERNS = ((128, 1), (512, 4), (2048, 16))
H_K_B, H_V_B, DK_B, DV_B, CONV_K = 4, 8, 128, 128, 4
H_C, HKV_C, DH_C, SWA_WINDOW = 8, 2, 64, 128
ATT_BLOCK = 128
ATT_TILE = 2 * ATT_BLOCK
A_W = H_A * DH_A
BK_W = H_K_B * DK_B
BV_W = H_V_B * DV_B
C_Q_W = H_C * DH_C
C_KV_W = HKV_C * DH_C
CONV_CH = 2 * BK_W + BV_W
PA_W = 3 * A_W
PB_GATE_OFF = CONV_CH + BV_W
PB_W = PB_GATE_OFF + LANES
PC_KV_W = 2 * C_KV_W
PC_W = C_Q_W + 2 * PC_KV_W
N_PERM = PA_W + PB_W + PC_W
N_GROUPS, EXPERTS_PER_GROUP, TOP_K, D_FF = 4, 8, 2, 512
N_EXPERTS = N_GROUPS * EXPERTS_PER_GROUP
DN_CHUNK = 128


def _cparams(semantics, vmem_mib):
    return pltpu.CompilerParams(dimension_semantics=semantics, vmem_limit_bytes=vmem_mib * MIB)


def _silu(x):
    return x * jax.nn.sigmoid(x)


def _ada_kernel(c_ref, w_ref, b_ref, o_ref):
    c = c_ref[...]
    ca = _silu(c).astype(BF16)
    o_ref[0] = jnp.dot(ca, w_ref[0].astype(BF16), preferred_element_type=F32) + b_ref[0]


def ada_modulation(c, ada_w, ada_b):
    L, D, N = ada_w.shape
    B = c.shape[0]
    cp = jnp.zeros((SUBLANES, D), F32).at[:B].set(c)
    tn = 1024
    return pl.pallas_call(
        _ada_kernel,
        grid=(L, N // tn),
        in_specs=[pl.BlockSpec((SUBLANES, D), lambda l, j: (0, 0)),
                  pl.BlockSpec((1, D, tn), lambda l, j: (l, 0, j)),
                  pl.BlockSpec((1, 1, tn), lambda l, j: (l, 0, j))],
        out_specs=pl.BlockSpec((1, SUBLANES, tn), lambda l, j: (l, 0, j)),
        out_shape=jax.ShapeDtypeStruct((L, SUBLANES, N), F32),
        compiler_params=_cparams(("arbitrary", "arbitrary"), 40),
        name="ada_modulation",
    )(cp, ada_w, ada_b.reshape(L, 1, N))


def _inproj_kernel(x_ref, g_ref, sc_ref, sh_ref, w_ref, *rest, dilations):
    oa_refs, (ob_ref, oc_ref, ra_s) = rest[:len(dilations)], rest[len(dilations):]
    x = x_ref[...]
    tm = x.shape[0]
    y = x * lax.rsqrt(jnp.mean(x * x, axis=-1, keepdims=True) + EPS) * g_ref[...]
    hb = (y * (1.0 + sc_ref[0]) + sh_ref[0]).astype(BF16)
    ra = jnp.dot(hb, w_ref[0, :, 0:PA_W], preferred_element_type=F32)
    ob_ref[...] = jnp.dot(hb, w_ref[0, :, PA_W:PA_W + PB_W], preferred_element_type=F32)
    oc_ref[...] = jnp.dot(hb, w_ref[0, :, PA_W + PB_W:N_PERM], preferred_element_type=F32).astype(BF16)
    n_ct = PA_W // LANES
    for c in range(n_ct):
        ra_s[c] = ra[:, c * LANES:(c + 1) * LANES]
    for d, oa_ref in zip(dilations, oa_refs):
        if d == 1:
            oa_ref[0] = ra.astype(BF16)
            continue
        for r in range(d):
            rows = [ra_s[c, pl.ds(r, tm // d, stride=d), :] for c in range(n_ct)]
            oa_ref[0, :, r * PA_W:(r + 1) * PA_W] = jnp.concatenate(rows, axis=-1).astype(BF16)


def input_projection(x, norm_g, mod, w_perm, layer, batch, seq, dilations):
    T, D = x.shape
    tm = 256
    per_b = seq // tm
    assert all(tm % (2 * SUBLANES * d) == 0 for d in dilations)
    return pl.pallas_call(
        functools.partial(_inproj_kernel, dilations=dilations),
        grid=(T // tm,),
        in_specs=[pl.BlockSpec((tm, D), lambda i: (i, 0)),
                  pl.BlockSpec((1, D), lambda i: (0, 0)),
                  pl.BlockSpec((1, 1, D), lambda i: (i // per_b, 0, 1)),
                  pl.BlockSpec((1, 1, D), lambda i: (i // per_b, 0, 0)),
                  pl.BlockSpec((1, D, N_PERM), lambda i: (layer, 0, 0), pipeline_mode=pl.Buffered(1))],
        out_specs=[pl.BlockSpec((1, tm // d, d * PA_W), lambda i: (i // per_b, i % per_b, 0)) for d in dilations]
        + [pl.BlockSpec((tm, PB_W), lambda i: (i, 0)),
           pl.BlockSpec((tm, PC_W), lambda i: (i, 0))],
        out_shape=[jax.ShapeDtypeStruct((batch, seq // d, d * PA_W), BF16) for d in dilations]
        + [jax.ShapeDtypeStruct((T, PB_W), F32),
           jax.ShapeDtypeStruct((T, PC_W), BF16)],
        scratch_shapes=[pltpu.VMEM((PA_W // LANES, tm, LANES), F32)],
        compiler_params=_cparams(("arbitrary",), 56),
        name="input_projection",
    )(x, norm_g.reshape(1, D), mod, mod, w_perm)


def _attn_kernel(*refs, n_pairs, pairs_per_kv, dh, scale, has_sink, want_lse):
    q_ref, kp_ref, kc_ref, vp_ref, vc_ref, bias_ref = refs[:6]
    pos = 6
    sink_ref = None
    if has_sink:
        sink_ref = refs[pos]
        pos += 1
    o_ref = refs[pos]
    lse_ref = refs[pos + 1] if want_lse else None
    B = ATT_BLOCK

    first = pl.program_id(2) == 0
    key_col = lax.broadcasted_iota(jnp.int32, (1, 2 * B), 1)
    first_pen = jnp.where((key_col < B) & first, -jnp.inf, 0.0).astype(F32)
    lane = lax.broadcasted_iota(jnp.int32, (B, LANES), 1)
    lo = lane < dh
    lo16 = lo.astype(F32).astype(BF16)
    hi16 = (lane >= dh).astype(F32).astype(BF16)

    units = [(sb, j) for sb in range(ATT_TILE // B) for j in range(n_pairs)]

    def kv_rows(ref_p, ref_c, sb, j):
        sl = slice((j // pairs_per_kv) * LANES, (j // pairs_per_kv + 1) * LANES)
        if sb == 0:
            return jnp.concatenate([ref_p[0, :, sl], ref_c[0, 0:B, sl]], axis=0)
        return ref_c[0, (sb - 1) * B:(sb + 1) * B, sl]

    qq, kk, vv = [], [], []
    for sb, j in units:
        q2 = q_ref[0, sb * B:(sb + 1) * B, j * LANES:(j + 1) * LANES] * scale
        qq.append(jnp.concatenate([q2 * lo16, q2 * hi16], axis=0))
        kk.append(kv_rows(kp_ref, kc_ref, sb, j))
        vv.append(kv_rows(vp_ref, vc_ref, sb, j))
    dn = (((1,), (1,)), ((), ()))
    s = [lax.dot_general(qq[u], kk[u], dn, preferred_element_type=F32) for u in range(len(units))]
    s = [s[u] + (bias_ref[j] + first_pen if sb == 0 else bias_ref[j]) for u, (sb, j) in enumerate(units)]
    m = [jnp.max(jnp.maximum(x[:, :B], x[:, B:]), axis=-1, keepdims=True) for x in s]
    sk = None
    if has_sink:
        sk = [jnp.concatenate([jnp.broadcast_to(sink_ref[2 * j][:, 0:1], (B, 1)),
                               jnp.broadcast_to(sink_ref[2 * j + 1][:, 0:1], (B, 1))], axis=0)
              for (_, j) in units]
        m = [jnp.maximum(m[u], sk[u]) for u in range(len(units))]
    p = [jnp.exp(s[u] - m[u]) for u in range(len(units))]
    den = [jnp.sum(x[:, :B] + x[:, B:], axis=-1, keepdims=True) for x in p]
    if has_sink:
        den = [den[u] + jnp.exp(sk[u] - m[u]) for u in range(len(units))]
    acc = [jnp.dot(p[u].astype(BF16), vv[u], preferred_element_type=F32) / den[u] for u in range(len(units))]
    for u, (sb, j) in enumerate(units):
        o_ref[0, sb * B:(sb + 1) * B, j * LANES:(j + 1) * LANES] = jnp.where(
            lo, acc[u][:B], acc[u][B:]).astype(o_ref.dtype)
    if want_lse:
        for sb in range(ATT_TILE // B):
            tile = jnp.zeros((B, LANES), F32)
            for u, (sb_u, j) in enumerate(units):
                if sb_u == sb:
                    lse = m[u] + jnp.log(den[u])
                    tile = jnp.where(lane == 2 * j, lse[:B], jnp.where(lane == 2 * j + 1, lse[B:], tile))
            lse_ref[0, sb * B:(sb + 1) * B, :] = tile


def _band_bias(slopes, dist_scale, max_dist):
    rel = (jnp.arange(ATT_BLOCK)[:, None] + ATT_BLOCK) - jnp.arange(2 * ATT_BLOCK)[None, :]
    valid = (rel >= 0) & (rel <= max_dist)
    bias = -(slopes.astype(F32) * dist_scale)[:, None, None] * rel.astype(F32)[None]
    bias = jnp.where(valid[None], bias, -jnp.inf)
    return bias.reshape(slopes.shape[0] // 2, 2 * ATT_BLOCK, 2 * ATT_BLOCK)


def banded_attention(view, batch, seq, dilation, *, n_pairs, pairs_per_kv, dh, q_off, k_off, v_off, width,
                     bias, sinks=None, want_lse=False):
    d = dilation
    ld = seq // d
    assert ld % ATT_TILE == 0 and 2 * dh == LANES and view.shape == (batch, ld, d * width)
    qw, kw = n_pairs * LANES, (n_pairs // pairs_per_kv) * LANES
    qb, kb, vb = q_off // qw, k_off // kw, v_off // kw
    assert q_off % qw == 0 and k_off % kw == 0 and v_off % kw == 0
    assert d == 1 or (width % qw == 0 and width % kw == 0)
    wb, wkb = width // qw, width // kw
    prev = lambda i: jnp.maximum(2 * i - 1, 0)
    in_specs = [
        pl.BlockSpec((1, ATT_TILE, qw), lambda b, r, i: (b, i, r * wb + qb)),
        pl.BlockSpec((1, ATT_BLOCK, kw), lambda b, r, i: (b, prev(i), r * wkb + kb)),
        pl.BlockSpec((1, ATT_TILE, kw), lambda b, r, i: (b, i, r * wkb + kb)),
        pl.BlockSpec((1, ATT_BLOCK, kw), lambda b, r, i: (b, prev(i), r * wkb + vb)),
        pl.BlockSpec((1, ATT_TILE, kw), lambda b, r, i: (b, i, r * wkb + vb)),
        pl.BlockSpec((n_pairs, 2 * ATT_BLOCK, 2 * ATT_BLOCK), lambda b, r, i: (0, 0, 0)),
    ]
    args = [view, view, view, view, view, bias]
    if sinks is not None:
        in_specs.append(pl.BlockSpec((2 * n_pairs, 1, LANES), lambda b, r, i: (0, 0, 0)))
        args.append(jnp.broadcast_to(sinks.astype(F32)[:, None, None], (2 * n_pairs, 1, LANES)))
    out_specs = [pl.BlockSpec((1, ATT_TILE, qw), lambda b, r, i: (b, i, r))]
    out_shape = [jax.ShapeDtypeStruct((batch, ld, d * qw), BF16)]
    if want_lse:
        out_specs.append(pl.BlockSpec((1, ATT_TILE, LANES), lambda b, r, i: (b, i, r)))
        out_shape.append(jax.ShapeDtypeStruct((batch, ld, d * LANES), F32))
    res = pl.pallas_call(
        functools.partial(_attn_kernel, n_pairs=n_pairs, pairs_per_kv=pairs_per_kv, dh=dh, scale=dh ** -0.5,
                          has_sink=sinks is not None, want_lse=want_lse),
        grid=(batch, d, ld // ATT_TILE),
        in_specs=in_specs, out_specs=out_specs, out_shape=out_shape,
        compiler_params=_cparams(("arbitrary", "arbitrary", "arbitrary"), 40),
        name=f"banded_attention_d{d}",
    )(*args)
    return res if want_lse else res[0]


def _merge_kernel(*refs, dilations):
    n = len(dilations)
    o_refs, l_refs, e_ref, out_ref = refs[:n], refs[n:2 * n], refs[2 * n], refs[2 * n + 1]
    o_s, l_s = refs[2 * n + 2], refs[2 * n + 3]
    e = e_ref[...]
    tm, W = out_ref.shape

    def natural(ref, scratch, d, width):
        if d == 1:
            return ref[0].astype(F32)
        n_ct = width // LANES
        for r in range(d):
            for c in range(n_ct):
                col = r * width + c * LANES
                scratch[c, pl.ds(r, tm // d, stride=d), :] = ref[0, :, col:col + LANES].astype(F32)
        return jnp.concatenate([scratch[c] for c in range(n_ct)], axis=-1)

    def expand(l):
        hi = l.astype(BF16)
        lo = (l - hi.astype(F32)).astype(BF16)
        return jnp.dot(hi, e, preferred_element_type=F32) + jnp.dot(lo, e, preferred_element_type=F32)

    os_, ls = [], []
    for k, d in enumerate(dilations):
        os_.append(natural(o_refs[k], o_s.at[k], d, W))
        ls.append(expand(natural(l_refs[k], l_s.at[k], d, LANES)))
    m = functools.reduce(jnp.maximum, ls)
    ws = [jnp.exp(l - m) for l in ls]
    num = sum(w * o for w, o in zip(ws, os_))
    out_ref[...] = (num / sum(ws)).astype(out_ref.dtype)


def merge_dilated(outs, lses, batch, seq, dilations):
    W = outs[0].shape[-1] // dilations[0]
    tm = 512
    per_b = seq // tm
    n = len(dilations)
    expand = (jnp.arange(LANES)[:, None] == (jnp.arange(W)[None, :] // DH_A)).astype(BF16)
    view_spec = lambda d, w: pl.BlockSpec((1, tm // d, d * w), lambda i: (i // per_b, i % per_b, 0))
    return pl.pallas_call(
        functools.partial(_merge_kernel, dilations=dilations),
        grid=(batch * per_b,),
        in_specs=[view_spec(d, W) for d in dilations] + [view_spec(d, LANES) for d in dilations]
        + [pl.BlockSpec((LANES, W), lambda i: (0, 0))],
        out_specs=pl.BlockSpec((tm, W), lambda i: (i, 0)),
        out_shape=jax.ShapeDtypeStruct((batch * seq, W), BF16),
        scratch_shapes=[pltpu.VMEM((n, W // LANES, tm, LANES), F32), pltpu.VMEM((n, 1, tm, LANES), F32)],
        compiler_params=_cparams(("arbitrary",), 32),
        name="merge_dilated",
    )(*outs, *lses, expand)


def _mm(a, b):
    return jnp.dot(a.astype(BF16), b.astype(BF16), preferred_element_type=F32)


def _dn_kernel(pb_ref, cw_ref, gpar_ref, ng_ref, lvl_ref, o_ref, xbuf, state, qs, ks, vs, gbs, obuf, *, tb):
    C = DN_CHUNK
    i = pl.program_id(1)

    @pl.when(i == 0)
    def _():
        xbuf[0:SUBLANES, :] = jnp.zeros((SUBLANES, CONV_CH), F32)
        state[...] = jnp.zeros_like(state)

    xbuf[SUBLANES:SUBLANES + tb, :] = pb_ref[0, :, 0:CONV_CH]
    y = cw_ref[CONV_K - 1:CONV_K, :] * xbuf[SUBLANES:SUBLANES + tb, :]
    for j in range(CONV_K - 1):
        off = SUBLANES - (CONV_K - 1) + j
        y = y + cw_ref[j:j + 1, :] * xbuf[off:off + tb, :]
    xbuf[0:SUBLANES, :] = xbuf[tb:tb + SUBLANES, :]
    qkv = _silu(y)

    def l2n(t):
        return t * lax.rsqrt(jnp.sum(t * t, axis=-1, keepdims=True) + EPS)

    for kh in range(H_K_B):
        sl = slice(kh * DK_B, (kh + 1) * DK_B)
        qs[:, sl] = l2n(qkv[:, sl]) * (DK_B ** -0.5)
        ks[:, sl] = l2n(qkv[:, BK_W + kh * DK_B:BK_W + (kh + 1) * DK_B])
    vs[...] = qkv[:, 2 * BK_W:]

    gl = pb_ref[0, :, PB_GATE_OFF:PB_GATE_OFF + LANES]
    lane = lax.broadcasted_iota(jnp.int32, (tb, LANES), 1)
    beta = jax.nn.sigmoid(gl)
    gdec = -jnp.exp(gpar_ref[0:1, :]) * jax.nn.softplus(gl + gpar_ref[1:2, :])
    gbs[...] = jnp.where(lane < H_V_B, beta, jnp.where(lane < 2 * H_V_B, gdec, 0.0))

    row = lax.broadcasted_iota(jnp.int32, (C, C), 0)
    col = lax.broadcasted_iota(jnp.int32, (C, C), 1)
    incl = row >= col
    strict = row > col
    tri = incl.astype(F32)
    eye = (row == col).astype(F32)
    heads = range(H_V_B)
    rep = H_V_B // H_K_B

    nc = tb // C
    kunits = [(ci, kh) for ci in range(nc) for kh in range(H_K_B)]
    units = [(ci, h) for ci in range(nc) for h in heads]
    rows = lambda ci: slice(ci * C, (ci + 1) * C)
    gb_c = [gbs[rows(ci), :] for ci in range(nc)]
    gc_c = [jnp.dot(tri, g, precision=HIGHEST, preferred_element_type=F32) for g in gb_c]
    gc_t = [g.T for g in gc_c]
    k_c = {(ci, kh): ks[rows(ci), kh * DK_B:(kh + 1) * DK_B] for ci, kh in kunits}
    q_c = {(ci, kh): qs[rows(ci), kh * DK_B:(kh + 1) * DK_B] for ci, kh in kunits}
    k_t = {u: k_c[u].T for u in kunits}
    gq = {u: _mm(jnp.concatenate([k_c[u], q_c[u]], axis=0), k_t[u]) for u in kunits}
    kof = lambda u: (u[0], u[1] // rep)
    beta_col = {(ci, h): gb_c[ci][:, h:h + 1] for ci, h in units}
    gcol = {(ci, h): gc_c[ci][:, H_V_B + h:H_V_B + h + 1] for ci, h in units}
    grow = {(ci, h): gc_t[ci][H_V_B + h:H_V_B + h + 1, :] for ci, h in units}
    decay = {u: jnp.exp(jnp.where(incl, gcol[u] - grow[u], -jnp.inf)) for u in units}
    lmb = {u: jnp.where(strict, gq[kof(u)][:C] * beta_col[u] * decay[u], 0.0).astype(BF16) for u in units}
    tinv = {u: eye - (lmb[u] * lvl_ref[0]).astype(F32) for u in units}
    for lvl in range(1, int(math.log2(C))):
        t16 = {u: tinv[u].astype(BF16) for u in units}
        a16 = {u: jnp.dot(t16[u], lmb[u] * lvl_ref[lvl], preferred_element_type=F32).astype(BF16) for u in units}
        tinv = {u: tinv[u] - jnp.dot(a16[u], t16[u], preferred_element_type=F32) for u in units}
    egc = {u: jnp.exp(gcol[u]) for u in units}
    rhs = {(ci, h): jnp.concatenate([vs[rows(ci), h * DV_B:(h + 1) * DV_B] * beta_col[ci, h],
                                     k_c[kof((ci, h))] * (beta_col[ci, h] * egc[ci, h])], axis=1)
           for ci, h in units}
    sol = {u: _mm(tinv[u], rhs[u]) for u in units}
    qk = {u: jnp.where(incl, gq[kof(u)][C:] * decay[u], 0.0).astype(BF16) for u in units}
    g_last = {u: grow[u][:, C - 1:C] for u in units}
    kd_t = {u: (k_t[kof(u)] * jnp.exp(g_last[u] - grow[u])).astype(BF16) for u in units}
    wq = {u: jnp.concatenate([sol[u][:, DV_B:], q_c[kof(u)] * egc[u]], axis=0).astype(BF16) for u in units}

    s_h = [state[h] for h in heads]
    for ci in range(nc):
        ws = [jnp.dot(wq[ci, h], s_h[h].astype(BF16), preferred_element_type=F32) for h in heads]
        v_new = [(sol[ci, h][:, :DV_B] - ws[h][:C]).astype(BF16) for h in heads]
        for h in heads:
            obuf[rows(ci), h * DV_B:(h + 1) * DV_B] = ws[h][C:] + jnp.dot(
                qk[ci, h], v_new[h], preferred_element_type=F32)
        s_h = [s_h[h] * jnp.exp(g_last[ci, h]) + jnp.dot(kd_t[ci, h], v_new[h], preferred_element_type=F32)
               for h in heads]
    for h in heads:
        state[h] = s_h[h]

    z = pb_ref[0, :, CONV_CH:CONV_CH + BV_W]
    ng = ng_ref[...]
    for hv in range(H_V_B):
        sl = slice(hv * DV_B, (hv + 1) * DV_B)
        o = obuf[:, sl]
        on = o * lax.rsqrt(jnp.mean(o * o, axis=-1, keepdims=True) + EPS) * ng
        o_ref[0, :, sl] = (on * _silu(z[:, sl])).astype(o_ref.dtype)


def gated_deltanet(pb, conv_w, a_log, dt_bias, norm_g, batch, seq):
    tb = 256
    view = pb.reshape(batch, seq, PB_W)
    gpar = jnp.zeros((2, LANES), F32)
    gpar = gpar.at[0, H_V_B:2 * H_V_B].set(a_log.astype(F32)).at[1, H_V_B:2 * H_V_B].set(dt_bias.astype(F32))
    n_lvl = int(math.log2(DN_CHUNK))
    ii = jnp.arange(DN_CHUNK)
    lvl_masks = jnp.stack([((ii[:, None] >> v) == (ii[None, :] >> v) + 1) & (((ii[:, None] >> v) & 1) == 1)
                           for v in range(n_lvl)]).astype(BF16)
    out = pl.pallas_call(
        functools.partial(_dn_kernel, tb=tb),
        grid=(batch, seq // tb),
        in_specs=[pl.BlockSpec((1, tb, PB_W), lambda b, i: (b, i, 0)),
                  pl.BlockSpec((CONV_K, CONV_CH), lambda b, i: (0, 0)),
                  pl.BlockSpec((2, LANES), lambda b, i: (0, 0)),
                  pl.BlockSpec((1, DV_B), lambda b, i: (0, 0)),
                  pl.BlockSpec((n_lvl, DN_CHUNK, DN_CHUNK), lambda b, i: (0, 0, 0))],
        out_specs=pl.BlockSpec((1, tb, BV_W), lambda b, i: (b, i, 0)),
        out_shape=jax.ShapeDtypeStruct((batch, seq, BV_W), BF16),
        scratch_shapes=[pltpu.VMEM((tb + SUBLANES, CONV_CH), F32),
                        pltpu.VMEM((H_V_B, DK_B, DV_B), F32),
                        pltpu.VMEM((tb, BK_W), F32),
                        pltpu.VMEM((tb, BK_W), F32),
                        pltpu.VMEM((tb, BV_W), F32),
                        pltpu.VMEM((tb, LANES), F32),
                        pltpu.VMEM((tb, BV_W), F32)],
        compiler_params=_cparams(("arbitrary", "arbitrary"), 48),
        name="gated_deltanet",
    )(view, conv_w, gpar, norm_g.reshape(1, DV_B).astype(F32), lvl_masks)
    return out.reshape(batch * seq, BV_W)


ROUTE_E1, ROUTE_E2, ROUTE_G1, ROUTE_G2 = 0, 1, 2, 3


def _pack_halves(hb):
    half = hb.shape[1] // 2
    lo = lax.bitcast_convert_type(hb[:, :half].astype(F32), jnp.uint32)
    hi = lax.bitcast_convert_type(hb[:, half:].astype(F32), jnp.uint32)
    return (hi & jnp.uint32(0xFFFF0000)) | (lo >> 16)


def _unpack_halves(w):
    lo = lax.bitcast_convert_type(w << 16, F32).astype(BF16)
    hi = lax.bitcast_convert_type(w & jnp.uint32(0xFFFF0000), F32).astype(BF16)
    return lo, hi


def _outproj_kernel(oa_ref, ob_ref, oc_ref, w_ref, x_ref, g1_ref, n2_ref, sc2_ref, sh2_ref, rw_ref, rb_ref,
                    x1_ref, hp_ref, route_ref, cnt_ref):
    i = pl.program_id(0)
    o = jnp.concatenate([oa_ref[...], ob_ref[...], oc_ref[...]], axis=-1)
    x1 = x_ref[...] + g1_ref[0] * jnp.dot(o, w_ref[0], preferred_element_type=F32)
    x1_ref[...] = x1
    h = x1 * lax.rsqrt(jnp.mean(x1 * x1, axis=-1, keepdims=True) + EPS) * n2_ref[...]
    h = h * (1.0 + sc2_ref[0]) + sh2_ref[0]
    h_hi = h.astype(BF16)
    hp_ref[...] = _pack_halves(h_hi)

    h_lo = (h - h_hi.astype(F32)).astype(BF16)
    hw = jnp.dot(h_hi, rw_ref[0], preferred_element_type=F32)
    logits = hw[:, :LANES] + hw[:, LANES:] + jnp.dot(h_lo, rw_ref[0, :, 0:LANES], preferred_element_type=F32) \
        + rb_ref[...]
    tm = logits.shape[0]
    lane = lax.broadcasted_iota(jnp.int32, (tm, LANES), 1).astype(F32)
    big = float(LANES)
    glane = (lane >= N_EXPERTS) & (lane < N_EXPERTS + N_GROUPS)
    gmax = jnp.max(jnp.where(glane, logits, -jnp.inf), axis=-1, keepdims=True)
    gsel = jnp.min(jnp.where(glane & (logits == gmax), lane, big), axis=-1, keepdims=True) - N_EXPERTS
    pg = 1.0 / jnp.sum(jnp.where(glane, jnp.exp(logits - gmax), 0.0), axis=-1, keepdims=True)
    emask = (lane >= gsel * EXPERTS_PER_GROUP) & (lane < (gsel + 1.0) * EXPERTS_PER_GROUP)
    v1 = jnp.max(jnp.where(emask, logits, -jnp.inf), axis=-1, keepdims=True)
    i1 = jnp.min(jnp.where(emask & (logits == v1), lane, big), axis=-1, keepdims=True)
    m2 = emask & (lane != i1)
    v2 = jnp.max(jnp.where(m2, logits, -jnp.inf), axis=-1, keepdims=True)
    i2 = jnp.min(jnp.where(m2 & (logits == v2), lane, big), axis=-1, keepdims=True)
    t = jnp.exp(v2 - v1)
    p1 = 1.0 / (1.0 + t)
    p2 = t / (1.0 + t)
    route_ref[...] = jnp.where(lane == ROUTE_E1, i1, jnp.where(lane == ROUTE_E2, i2, jnp.where(
        lane == ROUTE_G1, pg * p1, jnp.where(lane == ROUTE_G2, pg * p2, 0.0))))

    @pl.when(i == 0)
    def _():
        cnt_ref[...] = jnp.zeros_like(cnt_ref)

    onehot = ((lane == i1) | (lane == i2)).astype(F32)
    cnt_ref[...] += jnp.broadcast_to(jnp.sum(onehot, axis=0, keepdims=True), cnt_ref.shape)


def output_projection(oa, ob, oc, w_out, x, mod, norm2_g, rw2, rb, layer, seq):
    T, D = x.shape
    tm = 256
    per_b = seq // tm
    modspec = lambda col: pl.BlockSpec((1, 1, D), lambda i: (i // per_b, 0, col))
    return pl.pallas_call(
        _outproj_kernel,
        grid=(T // tm,),
        in_specs=[pl.BlockSpec((tm, A_W), lambda i: (i, 0)),
                  pl.BlockSpec((tm, BV_W), lambda i: (i, 0)),
                  pl.BlockSpec((tm, C_Q_W), lambda i: (i, 0)),
                  pl.BlockSpec((1, D, D), lambda i: (layer, 0, 0)),
                  pl.BlockSpec((tm, D), lambda i: (i, 0)),
                  modspec(2),
                  pl.BlockSpec((1, D), lambda i: (0, 0)),
                  modspec(4), modspec(3),
                  pl.BlockSpec((1, D, 2 * LANES), lambda i: (layer, 0, 0)),
                  pl.BlockSpec((1, LANES), lambda i: (0, 0))],
        out_specs=[pl.BlockSpec((tm, D), lambda i: (i, 0)),
                   pl.BlockSpec((tm, D // 2), lambda i: (i, 0)),
                   pl.BlockSpec((tm, LANES), lambda i: (i, 0)),
                   pl.BlockSpec((SUBLANES, LANES), lambda i: (0, 0))],
        out_shape=[jax.ShapeDtypeStruct((T, D), F32),
                   jax.ShapeDtypeStruct((T, D // 2), jnp.uint32),
                   jax.ShapeDtypeStruct((T, LANES), F32),
                   jax.ShapeDtypeStruct((SUBLANES, LANES), F32)],
        compiler_params=_cparams(("arbitrary",), 48),
        name="output_projection_router",
    )(oa, ob, oc, w_out, x, mod, norm2_g.reshape(1, D), mod, mod, rw2, rb)


EXPERT_TILE = 256


def _plan_kernel(route_ref, cnt_ref, pos_ref, be_ref, carry, pstart, *, tt, nblk_lanes):
    i = pl.program_id(0)
    r128 = lax.broadcasted_iota(jnp.int32, (LANES, LANES), 0)
    c128 = lax.broadcasted_iota(jnp.int32, (LANES, LANES), 1)

    @pl.when(i == 0)
    def _():
        cnt = cnt_ref[...]
        padded = jnp.ceil(cnt / EXPERT_TILE) * EXPERT_TILE
        upper = (r128 <= c128).astype(F32)
        pend = jnp.dot(padded, upper, precision=HIGHEST, preferred_element_type=F32)
        pstart[...] = pend - padded
        carry[...] = jnp.zeros_like(carry)
        pend_col = jnp.broadcast_to(pend[0:1, :], (LANES, LANES)).T
        ecol = r128 < N_EXPERTS
        n_used = pend[0:1, N_EXPERTS - 1:N_EXPERTS] / EXPERT_TILE
        for v in range(nblk_lanes // LANES):
            jlane = c128[0:1, :] + v * LANES
            jrow = ((c128 + v * LANES) * EXPERT_TILE).astype(F32)
            n = jnp.sum(jnp.where(ecol & (pend_col <= jrow), 1.0, 0.0), axis=0, keepdims=True)
            be = jnp.where(jlane == nblk_lanes - 1, n_used, jnp.minimum(n, float(N_EXPERTS - 1)))
            be_ref[:, v * LANES:(v + 1) * LANES] = jnp.broadcast_to(be.astype(jnp.int32), (SUBLANES, LANES))

    route = route_ref[...]
    lane = lax.broadcasted_iota(jnp.int32, (tt, LANES), 1).astype(F32)
    oh1 = lane == route[:, ROUTE_E1:ROUTE_E1 + 1]
    oh2 = lane == route[:, ROUTE_E2:ROUTE_E2 + 1]
    both = oh1.astype(F32) + oh2.astype(F32)
    rr = lax.broadcasted_iota(jnp.int32, (tt, tt), 0)
    cc = lax.broadcasted_iota(jnp.int32, (tt, tt), 1)
    before = jnp.dot((rr > cc).astype(BF16), both.astype(BF16), preferred_element_type=F32) + carry[0:1, :]
    base = before + pstart[0:1, :]
    p1 = jnp.sum(jnp.where(oh1, base, 0.0), axis=-1, keepdims=True)
    p2 = jnp.sum(jnp.where(oh2, base, 0.0), axis=-1, keepdims=True)
    pos_ref[...] = jnp.where(lane == 0.0, p1, jnp.where(lane == 1.0, p2, 0.0)).astype(jnp.int32)
    carry[...] += jnp.broadcast_to(jnp.sum(both, axis=0, keepdims=True), carry.shape)


def routing_plan(route, cnt, n_blocks):
    T = route.shape[0]
    tt = 512
    nblk_lanes = -(-n_blocks // LANES) * LANES
    return pl.pallas_call(
        functools.partial(_plan_kernel, tt=tt, nblk_lanes=nblk_lanes),
        grid=(T // tt,),
        in_specs=[pl.BlockSpec((tt, LANES), lambda i: (i, 0)),
                  pl.BlockSpec((SUBLANES, LANES), lambda i: (0, 0))],
        out_specs=[pl.BlockSpec((tt, LANES), lambda i: (i, 0)),
                   pl.BlockSpec((SUBLANES, nblk_lanes), lambda i: (0, 0))],
        out_shape=[jax.ShapeDtypeStruct((T, LANES), jnp.int32),
                   jax.ShapeDtypeStruct((SUBLANES, nblk_lanes), jnp.int32)],
        scratch_shapes=[pltpu.VMEM((SUBLANES, LANES), F32), pltpu.VMEM((SUBLANES, LANES), F32)],
        compiler_params=_cparams(("arbitrary",), 32),
        name="routing_plan",
    )(route, cnt)


def _row_copy(src_ref, dst_ref, src_row, dst_row, sem):
    return pltpu.make_async_copy(src_ref.at[pl.ds(src_row, 1)], dst_ref.at[pl.ds(dst_row, 1)], sem)


def _dispatch_kernel(pos_ref, hp_ref, xs_in_ref, xs_ref, sem, *, td):
    del xs_in_ref
    base = pl.program_id(0) * td

    def issue(r, c):
        for k in range(TOP_K):
            _row_copy(hp_ref, xs_ref, r, pos_ref[(base + r) * TOP_K + k], sem).start(priority=k)
        return c

    lax.fori_loop(0, td, issue, 0, unroll=8)

    def drain(r, c):
        for k in range(TOP_K):
            _row_copy(hp_ref, xs_ref, 0, 0, sem).wait()
        return c

    lax.fori_loop(0, td, drain, 0, unroll=8)


def dispatch_rows(pos_flat, hp, n_rows):
    T, W = hp.shape
    td = 512
    return pl.pallas_call(
        functools.partial(_dispatch_kernel, td=td),
        grid_spec=pltpu.PrefetchScalarGridSpec(
            num_scalar_prefetch=1, grid=(T // td,),
            in_specs=[pl.BlockSpec((td, W), lambda i, pos: (i, 0)),
                      pl.BlockSpec(memory_space=pl.ANY)],
            out_specs=pl.BlockSpec(memory_space=pl.ANY),
            scratch_shapes=[pltpu.SemaphoreType.DMA(())]),
        out_shape=jax.ShapeDtypeStruct((n_rows, W), hp.dtype),
        input_output_aliases={2: 0},
        compiler_params=_cparams(("arbitrary",), 32),
        name="moe_dispatch",
    )(pos_flat, hp, jnp.zeros((n_rows, W), hp.dtype))


def _expert_kernel(be_ref, xs_ref, wg_ref, wu_ref, wd_ref, y_ref, wgb, wub, wdb):
    j = pl.program_id(0)
    n_used = be_ref[be_ref.shape[0] - 1]

    @pl.when(j < n_used)
    def _():
        @pl.when((j == 0) | (be_ref[j] != be_ref[jnp.maximum(j - 1, 0)]))
        def _():
            wgb[...] = wg_ref[0, 0].astype(BF16)
            wub[...] = wu_ref[0, 0].astype(BF16)
            wdb[...] = wd_ref[0, 0].astype(BF16)

        lo, hi = _unpack_halves(xs_ref[...])
        half = lo.shape[1]
        gate = jnp.dot(lo, wgb[0:half, :], preferred_element_type=F32) \
            + jnp.dot(hi, wgb[half:, :], preferred_element_type=F32)
        up = jnp.dot(lo, wub[0:half, :], preferred_element_type=F32) \
            + jnp.dot(hi, wub[half:, :], preferred_element_type=F32)
        hid = (_silu(gate) * up).astype(BF16)
        y_ref[...] = jnp.dot(hid, wdb[...], preferred_element_type=F32)

    @pl.when(j >= n_used)
    def _():
        y_ref[...] = jnp.zeros_like(y_ref)


def expert_mlp(block_plan, xs, w_gate, w_up, w_down, layer):
    n_rows, W = xs.shape
    _, E, D, F = w_gate.shape
    n_blocks = n_rows // EXPERT_TILE
    assert n_blocks < block_plan.shape[0]
    last = block_plan.shape[0] - 1
    used = lambda j, be: jnp.minimum(j, be[last] - 1)
    wspec = lambda shape: pl.BlockSpec((1, 1) + shape, lambda j, be: (layer, be[used(j, be)], 0, 0))
    return pl.pallas_call(
        _expert_kernel,
        grid_spec=pltpu.PrefetchScalarGridSpec(
            num_scalar_prefetch=1, grid=(n_blocks,),
            in_specs=[pl.BlockSpec((EXPERT_TILE, W), lambda j, be: (used(j, be), 0)),
                      wspec((D, F)), wspec((D, F)), wspec((F, D))],
            out_specs=pl.BlockSpec((EXPERT_TILE, D), lambda j, be: (j, 0)),
            scratch_shapes=[pltpu.VMEM((D, F), BF16), pltpu.VMEM((D, F), BF16), pltpu.VMEM((F, D), BF16)]),
        out_shape=jax.ShapeDtypeStruct((n_rows, D), F32),
        compiler_params=_cparams(("arbitrary",), 56),
        name="expert_mlp",
    )(block_plan, xs, w_gate, w_up, w_down)


def _combine_kernel(pos_ref, y_ref, x1_ref, g2_ref, route_ref, fg_ref, out_ref, ybuf, sem, *, tc, final_norm):
    base = pl.program_id(0) * tc

    def issue(r, c):
        for k in range(TOP_K):
            _row_copy(y_ref, ybuf.at[k], pos_ref[(base + r) * TOP_K + k], r, sem).start(priority=k)
        return c

    lax.fori_loop(0, tc, issue, 0, unroll=8)

    def drain(r, c):
        for k in range(TOP_K):
            _row_copy(y_ref, ybuf.at[k], 0, 0, sem).wait()
        return c

    lax.fori_loop(0, tc, drain, 0, unroll=8)

    route = route_ref[...]
    moe = route[:, ROUTE_G1:ROUTE_G1 + 1] * ybuf[0] + route[:, ROUTE_G2:ROUTE_G2 + 1] * ybuf[1]
    x2 = x1_ref[...] + g2_ref[0] * moe
    if final_norm:
        x2 = x2 * lax.rsqrt(jnp.mean(x2 * x2, axis=-1, keepdims=True) + EPS) * fg_ref[...]
    out_ref[...] = x2


def combine_rows(pos_flat, y, x1, mod, route, final_g, seq, final_norm):
    T, D = x1.shape
    tc = 512
    per_b = seq // tc
    return pl.pallas_call(
        functools.partial(_combine_kernel, tc=tc, final_norm=final_norm),
        grid_spec=pltpu.PrefetchScalarGridSpec(
            num_scalar_prefetch=1, grid=(T // tc,),
            in_specs=[pl.BlockSpec(memory_space=pl.ANY),
                      pl.BlockSpec((tc, D), lambda i, pos: (i, 0)),
                      pl.BlockSpec((1, 1, D), lambda i, pos: (i // per_b, 0, 5)),
                      pl.BlockSpec((tc, LANES), lambda i, pos: (i, 0)),
                      pl.BlockSpec((1, D), lambda i, pos: (0, 0))],
            out_specs=pl.BlockSpec((tc, D), lambda i, pos: (i, 0)),
            scratch_shapes=[pltpu.VMEM((TOP_K, tc, D), F32), pltpu.SemaphoreType.DMA(())]),
        out_shape=jax.ShapeDtypeStruct((T, D), F32),
        compiler_params=_cparams(("arbitrary",), 40),
        name="moe_combine",
    )(pos_flat, y, x1, mod, route, final_g.reshape(1, D))


def _alibi_slopes(n):
    return 2.0 ** (-8.0 * jnp.arange(1, n + 1, dtype=F32) / n)


def _permute_w_in(w_in):
    L, D, _ = w_in.shape
    g_end = PA_W + PB_GATE_OFF + 2 * H_V_B
    pad = jnp.zeros((L, D, LANES - 2 * H_V_B), w_in.dtype)
    cq = w_in[:, :, g_end:g_end + C_Q_W]
    ck = w_in[:, :, g_end + C_Q_W:g_end + C_Q_W + C_KV_W].reshape(L, D, HKV_C, DH_C)
    cv = w_in[:, :, g_end + C_Q_W + C_KV_W:].reshape(L, D, HKV_C, DH_C)
    dup = lambda t: jnp.concatenate([t, t], axis=-1).reshape(L, D, PC_KV_W)
    return jnp.concatenate([w_in[:, :, :g_end], pad, cq, dup(ck), dup(cv)], axis=-1).astype(BF16)


def kernel(x, c, norm1_g, norm2_g, ada_w, ada_b, w_in, dn_conv_w, dn_a_log, dn_dt_bias, dn_norm_g, attn_sinks,
           w_out, router_group_w, router_group_b, router_expert_w, router_expert_b, expert_w_gate, expert_w_up,
           expert_w_down, final_norm_g):
    B, S, D = x.shape
    L = ada_w.shape[0]
    T = B * S
    n_rows = T * TOP_K + N_EXPERTS * EXPERT_TILE
    n_blocks = n_rows // EXPERT_TILE

    mod_all = ada_modulation(c, ada_w, ada_b)[:, :B].reshape(L, B, 1, 6 * D)
    w_perm = _permute_w_in(w_in)
    w_out_b = w_out.astype(BF16)
    rw = jnp.concatenate([router_expert_w, router_group_w,
                          jnp.zeros((L, D, LANES - N_EXPERTS - N_GROUPS), F32)], axis=-1)
    rw_hi = rw.astype(BF16)
    rw2 = jnp.concatenate([rw_hi, (rw - rw_hi.astype(F32)).astype(BF16)], axis=-1)
    rb = jnp.concatenate([router_expert_b, router_group_b,
                          jnp.zeros((L, LANES - N_EXPERTS - N_GROUPS), F32)], axis=-1).reshape(L, 1, LANES)
    slopes_a = _alibi_slopes(H_A)
    bias_a = [_band_bias(slopes_a, float(dil), wdw // dil) for (wdw, dil) in DILATED_PATTERNS]
    bias_c = _band_bias(_alibi_slopes(H_C), 1.0, SWA_WINDOW - 1)

    dils = tuple(dil for (_, dil) in DILATED_PATTERNS)
    xt = x.reshape(T, D)
    for l in range(L):
        mod = mod_all[l]
        *pas, pb, pc = input_projection(xt, norm1_g[l], mod, w_perm, l, B, S, dils)
        outs, lses = [], []
        for pa_d, (wdw, dil), bias in zip(pas, DILATED_PATTERNS, bias_a):
            o, lse = banded_attention(pa_d, B, S, dil, n_pairs=H_A // 2, pairs_per_kv=1, dh=DH_A, q_off=0,
                                      k_off=A_W, v_off=2 * A_W, width=PA_W, bias=bias, want_lse=True)
            outs.append(o)
            lses.append(lse)
        o_a = merge_dilated(outs, lses, B, S, dils)
        o_b = gated_deltanet(pb, dn_conv_w[l], dn_a_log[l], dn_dt_bias[l], dn_norm_g[l], B, S)
        o_c = banded_attention(pc.reshape(B, S, PC_W), B, S, 1, n_pairs=H_C // 2, pairs_per_kv=H_C // HKV_C // 2,
                               dh=DH_C, q_off=0, k_off=C_Q_W, v_off=C_Q_W + PC_KV_W, width=PC_W, bias=bias_c,
                               sinks=attn_sinks[l]).reshape(T, C_Q_W)
        x1, hp, route, cnt = output_projection(o_a, o_b, o_c, w_out_b, xt, mod, norm2_g[l], rw2, rb[l], l, S)
        pos, be = routing_plan(route, cnt, n_blocks)
        pos_flat = pos[:, :TOP_K].reshape(T * TOP_K)
        xs = dispatch_rows(pos_flat, hp, n_rows)
        y = expert_mlp(be[0], xs, expert_w_gate, expert_w_up, expert_w_down, l)
        xt = combine_rows(pos_flat, y, x1, mod, route, final_norm_g, S, final_norm=(l == L - 1))
    return xt.reshape(B, S, D)
```

```python
import functools
import math

import jax
import jax.numpy as jnp
from jax import lax
from jax.experimental import pallas as pl
from jax.experimental.pallas import tpu as pltpu

F32 = jnp.float32
BF16 = jnp.bfloat16
HIGHEST = lax.Precision.HIGHEST

LANES = 128
SUBLANES = 8
MIB = 1024 * 1024

EPS = 1e-6
D_MODEL = 2048
H_A, DH_A = 8, 64
DILATED_PATTERNS = ((128, 1), (512, 4), (2048, 16))
H_K_B, H_V_B, DK_B, DV_B, CONV_K = 4, 8, 128, 128, 4
H_C, HKV_C, DH_C, SWA_WINDOW = 8, 2, 64, 128
ATT_BLOCK = 128
ATT_TILE = 2 * ATT_BLOCK
A_W = H_A * DH_A
BK_W = H_K_B * DK_B
BV_W = H_V_B * DV_B
C_Q_W = H_C * DH_C
C_KV_W = HKV_C * DH_C
CONV_CH = 2 * BK_W + BV_W
PA_W = 3 * A_W
PB_GATE_OFF = CONV_CH + BV_W
PB_W = PB_GATE_OFF + LANES
PC_KV_W = 2 * C_KV_W
PC_W = C_Q_W + 2 * PC_KV_W
N_PERM = PA_W + PB_W + PC_W
N_GROUPS, EXPERTS_PER_GROUP, TOP_K, D_FF = 4, 8, 2, 512
N_EXPERTS = N_GROUPS * EXPERTS_PER_GROUP
DN_CHUNK = 128


def _cparams(semantics, vmem_mib):
    return pltpu.CompilerParams(dimension_semantics=semantics, vmem_limit_bytes=vmem_mib * MIB)


def _silu(x):
    return x * jax.nn.sigmoid(x)


def _ada_kernel(c_ref, w_ref, b_ref, o_ref):
    c = c_ref[...]
    ca = _silu(c).astype(BF16)
    o_ref[0] = jnp.dot(ca, w_ref[0].astype(BF16), preferred_element_type=F32) + b_ref[0]


def ada_modulation(c, ada_w, ada_b):
    L, D, N = ada_w.shape
    B = c.shape[0]
    cp = jnp.zeros((SUBLANES, D), F32).at[:B].set(c)
    tn = 1024
    return pl.pallas_call(
        _ada_kernel,
        grid=(L, N // tn),
        in_specs=[pl.BlockSpec((SUBLANES, D), lambda l, j: (0, 0)),
                  pl.BlockSpec((1, D, tn), lambda l, j: (l, 0, j)),
                  pl.BlockSpec((1, 1, tn), lambda l, j: (l, 0, j))],
        out_specs=pl.BlockSpec((1, SUBLANES, tn), lambda l, j: (l, 0, j)),
        out_shape=jax.ShapeDtypeStruct((L, SUBLANES, N), F32),
        compiler_params=_cparams(("arbitrary", "arbitrary"), 40),
        name="ada_modulation",
    )(cp, ada_w, ada_b.reshape(L, 1, N))


def _inproj_kernel(x_ref, g_ref, sc_ref, sh_ref, w_ref, *rest, dilations):
    oa_refs, (ob_ref, oc_ref, ra_s) = rest[:len(dilations)], rest[len(dilations):]
    x = x_ref[...]
    tm = x.shape[0]
    y = x * lax.rsqrt(jnp.mean(x * x, axis=-1, keepdims=True) + EPS) * g_ref[...]
    hb = (y * (1.0 + sc_ref[0]) + sh_ref[0]).astype(BF16)
    ra = jnp.dot(hb, w_ref[0, :, 0:PA_W], preferred_element_type=F32)
    ob_ref[...] = jnp.dot(hb, w_ref[0, :, PA_W:PA_W + PB_W], preferred_element_type=F32)
    oc_ref[...] = jnp.dot(hb, w_ref[0, :, PA_W + PB_W:N_PERM], preferred_element_type=F32).astype(BF16)
    n_ct = PA_W // LANES
    for c in range(n_ct):
        ra_s[c] = ra[:, c * LANES:(c + 1) * LANES]
    for d, oa_ref in zip(dilations, oa_refs):
        if d == 1:
            oa_ref[0] = ra.astype(BF16)
            continue
        for r in range(d):
            rows = [ra_s[c, pl.ds(r, tm // d, stride=d), :] for c in range(n_ct)]
            oa_ref[0, :, r * PA_W:(r + 1) * PA_W] = jnp.concatenate(rows, axis=-1).astype(BF16)


def input_projection(x, norm_g, mod, w_perm, layer, batch, seq, dilations):
    T, D = x.shape
    tm = 256
    per_b = seq // tm
    assert all(tm % (2 * SUBLANES * d) == 0 for d in dilations)
    return pl.pallas_call(
        functools.partial(_inproj_kernel, dilations=dilations),
        grid=(T // tm,),
        in_specs=[pl.BlockSpec((tm, D), lambda i: (i, 0)),
                  pl.BlockSpec((1, D), lambda i: (0, 0)),
                  pl.BlockSpec((1, 1, D), lambda i: (i // per_b, 0, 1)),
                  pl.BlockSpec((1, 1, D), lambda i: (i // per_b, 0, 0)),
                  pl.BlockSpec((1, D, N_PERM), lambda i: (layer, 0, 0), pipeline_mode=pl.Buffered(1))],
        out_specs=[pl.BlockSpec((1, tm // d, d * PA_W), lambda i: (i // per_b, i % per_b, 0)) for d in dilations]
        + [pl.BlockSpec((tm, PB_W), lambda i: (i, 0)),
           pl.BlockSpec((tm, PC_W), lambda i: (i, 0))],
        out_shape=[jax.ShapeDtypeStruct((batch, seq // d, d * PA_W), BF16) for d in dilations]
        + [jax.ShapeDtypeStruct((T, PB_W), F32),
           jax.ShapeDtypeStruct((T, PC_W), BF16)],
        scratch_shapes=[pltpu.VMEM((PA_W // LANES, tm, LANES), F32)],
        compiler_params=_cparams(("arbitrary",), 56),
        name="input_projection",
    )(x, norm_g.reshape(1, D), mod, mod, w_perm)


def _attn_kernel(*refs, n_pairs, pairs_per_kv, dh, scale, has_sink, want_lse):
    q_ref, kp_ref, kc_ref, vp_ref, vc_ref, bias_ref = refs[:6]
    pos = 6
    sink_ref = None
    if has_sink:
        sink_ref = refs[pos]
        pos += 1
    o_ref = refs[pos]
    lse_ref = refs[pos + 1] if want_lse else None
    B = ATT_BLOCK

    first = pl.program_id(2) == 0
    key_col = lax.broadcasted_iota(jnp.int32, (1, 2 * B), 1)
    first_pen = jnp.where((key_col < B) & first, -jnp.inf, 0.0).astype(F32)
    lane = lax.broadcasted_iota(jnp.int32, (B, LANES), 1)
    lo = lane < dh
    lo16 = lo.astype(F32).astype(BF16)
    hi16 = (lane >= dh).astype(F32).astype(BF16)

    units = [(sb, j) for sb in range(ATT_TILE // B) for j in range(n_pairs)]

    def kv_rows(ref_p, ref_c, sb, j):
        sl = slice((j // pairs_per_kv) * LANES, (j // pairs_per_kv + 1) * LANES)
        if sb == 0:
            return jnp.concatenate([ref_p[0, :, sl], ref_c[0, 0:B, sl]], axis=0)
        return ref_c[0, (sb - 1) * B:(sb + 1) * B, sl]

    qq, kk, vv = [], [], []
    for sb, j in units:
        q2 = q_ref[0, sb * B:(sb + 1) * B, j * LANES:(j + 1) * LANES] * scale
        qq.append(jnp.concatenate([q2 * lo16, q2 * hi16], axis=0))
        kk.append(kv_rows(kp_ref, kc_ref, sb, j))
        vv.append(kv_rows(vp_ref, vc_ref, sb, j))
    dn = (((1,), (1,)), ((), ()))
    s = [lax.dot_general(qq[u], kk[u], dn, preferred_element_type=F32) for u in range(len(units))]
    s = [s[u] + (bias_ref[j] + first_pen if sb == 0 else bias_ref[j]) for u, (sb, j) in enumerate(units)]
    mx = [jnp.maximum(x[:, :B], x[:, B:]) for x in s]
    sk = None
    if has_sink:
        sk_pair = [jnp.concatenate([jnp.broadcast_to(sink_ref[2 * j], (B, LANES)),
                                    jnp.broadcast_to(sink_ref[2 * j + 1], (B, LANES))], axis=0)
                   for j in range(n_pairs)]
        sk = [sk_pair[j] for (_, j) in units]
        mx = [jnp.maximum(mx[u], sk[u]) for u in range(len(units))]
    m = [jnp.max(x, axis=-1, keepdims=True) for x in mx]
    p = [jnp.exp(s[u] - m[u]) for u in range(len(units))]
    psum = [x[:, :B] + x[:, B:] for x in p]
    if has_sink:
        lane2 = lax.broadcasted_iota(jnp.int32, (2 * B, LANES), 1)
        psum = [psum[u] + jnp.where(lane2 == 0, jnp.exp(sk[u] - m[u]), 0.0) for u in range(len(units))]
    den = [jnp.sum(x, axis=-1, keepdims=True) for x in psum]
    acc = [jnp.dot(p[u].astype(BF16), vv[u], preferred_element_type=F32) / den[u] for u in range(len(units))]
    for u, (sb, j) in enumerate(units):
        o_ref[0, sb * B:(sb + 1) * B, j * LANES:(j + 1) * LANES] = jnp.where(
            lo, acc[u][:B], acc[u][B:]).astype(o_ref.dtype)
    if want_lse:
        for sb in range(ATT_TILE // B):
            tile = jnp.zeros((B, LANES), F32)
            for u, (sb_u, j) in enumerate(units):
                if sb_u == sb:
                    lse = m[u] + jnp.log(den[u])
                    tile = jnp.where(lane == 2 * j, lse[:B], jnp.where(lane == 2 * j + 1, lse[B:], tile))
            lse_ref[0, sb * B:(sb + 1) * B, :] = tile


def _band_bias(slopes, dist_scale, max_dist):
    rel = (jnp.arange(ATT_BLOCK)[:, None] + ATT_BLOCK) - jnp.arange(2 * ATT_BLOCK)[None, :]
    valid = (rel >= 0) & (rel <= max_dist)
    bias = -(slopes.astype(F32) * dist_scale)[:, None, None] * rel.astype(F32)[None]
    bias = jnp.where(valid[None], bias, -jnp.inf)
    return bias.reshape(slopes.shape[0] // 2, 2 * ATT_BLOCK, 2 * ATT_BLOCK)


def banded_attention(view, batch, seq, dilation, *, n_pairs, pairs_per_kv, dh, q_off, k_off, v_off, width,
                     bias, sinks=None, want_lse=False):
    d = dilation
    ld = seq // d
    assert ld % ATT_TILE == 0 and 2 * dh == LANES and view.shape == (batch, ld, d * width)
    qw, kw = n_pairs * LANES, (n_pairs // pairs_per_kv) * LANES
    qb, kb, vb = q_off // qw, k_off // kw, v_off // kw
    assert q_off % qw == 0 and k_off % kw == 0 and v_off % kw == 0
    assert d == 1 or (width % qw == 0 and width % kw == 0)
    wb, wkb = width // qw, width // kw
    prev = lambda i: jnp.maximum(2 * i - 1, 0)
    in_specs = [
        pl.BlockSpec((1, ATT_TILE, qw), lambda b, r, i: (b, i, r * wb + qb)),
        pl.BlockSpec((1, ATT_BLOCK, kw), lambda b, r, i: (b, prev(i), r * wkb + kb)),
        pl.BlockSpec((1, ATT_TILE, kw), lambda b, r, i: (b, i, r * wkb + kb)),
        pl.BlockSpec((1, ATT_BLOCK, kw), lambda b, r, i: (b, prev(i), r * wkb + vb)),
        pl.BlockSpec((1, ATT_TILE, kw), lambda b, r, i: (b, i, r * wkb + vb)),
        pl.BlockSpec((n_pairs, 2 * ATT_BLOCK, 2 * ATT_BLOCK), lambda b, r, i: (0, 0, 0)),
    ]
    args = [view, view, view, view, view, bias]
    if sinks is not None:
        in_specs.append(pl.BlockSpec((2 * n_pairs, 1, LANES), lambda b, r, i: (0, 0, 0)))
        args.append(jnp.broadcast_to(sinks.astype(F32)[:, None, None], (2 * n_pairs, 1, LANES)))
    out_specs = [pl.BlockSpec((1, ATT_TILE, qw), lambda b, r, i: (b, i, r))]
    out_shape = [jax.ShapeDtypeStruct((batch, ld, d * qw), BF16)]
    if want_lse:
        out_specs.append(pl.BlockSpec((1, ATT_TILE, LANES), lambda b, r, i: (b, i, r)))
        out_shape.append(jax.ShapeDtypeStruct((batch, ld, d * LANES), F32))
    res = pl.pallas_call(
        functools.partial(_attn_kernel, n_pairs=n_pairs, pairs_per_kv=pairs_per_kv, dh=dh, scale=dh ** -0.5,
                          has_sink=sinks is not None, want_lse=want_lse),
        grid=(batch, d, ld // ATT_TILE),
        in_specs=in_specs, out_specs=out_specs, out_shape=out_shape,
        compiler_params=_cparams(("arbitrary", "arbitrary", "arbitrary"), 40),
        name=f"banded_attention_d{d}",
    )(*args)
    return res if want_lse else res[0]


def _merge_kernel(*refs, dilations):
    n = len(dilations)
    o_refs, l_refs, e_ref, out_ref = refs[:n], refs[n:2 * n], refs[2 * n], refs[2 * n + 1]
    o_s, l_s = refs[2 * n + 2], refs[2 * n + 3]
    e = e_ref[...]
    tm, W = out_ref.shape

    def natural(ref, scratch, d, width):
        if d == 1:
            return ref[0].astype(F32)
        n_ct = width // LANES
        for r in range(d):
            for c in range(n_ct):
                col = r * width + c * LANES
                scratch[c, pl.ds(r, tm // d, stride=d), :] = ref[0, :, col:col + LANES].astype(F32)
        return jnp.concatenate([scratch[c] for c in range(n_ct)], axis=-1)

    def expand(l):
        hi = l.astype(BF16)
        lo = (l - hi.astype(F32)).astype(BF16)
        return jnp.dot(hi, e, preferred_element_type=F32) + jnp.dot(lo, e, preferred_element_type=F32)

    os_, ls = [], []
    for k, d in enumerate(dilations):
        os_.append(natural(o_refs[k], o_s.at[k], d, W))
        ls.append(expand(natural(l_refs[k], l_s.at[k], d, LANES)))
    m = functools.reduce(jnp.maximum, ls)
    ws = [jnp.exp(l - m) for l in ls]
    num = sum(w * o for w, o in zip(ws, os_))
    out_ref[...] = (num / sum(ws)).astype(out_ref.dtype)


def merge_dilated(outs, lses, batch, seq, dilations):
    W = outs[0].shape[-1] // dilations[0]
    tm = 512
    per_b = seq // tm
    n = len(dilations)
    expand = (jnp.arange(LANES)[:, None] == (jnp.arange(W)[None, :] // DH_A)).astype(BF16)
    view_spec = lambda d, w: pl.BlockSpec((1, tm // d, d * w), lambda i: (i // per_b, i % per_b, 0))
    return pl.pallas_call(
        functools.partial(_merge_kernel, dilations=dilations),
        grid=(batch * per_b,),
        in_specs=[view_spec(d, W) for d in dilations] + [view_spec(d, LANES) for d in dilations]
        + [pl.BlockSpec((LANES, W), lambda i: (0, 0))],
        out_specs=pl.BlockSpec((tm, W), lambda i: (i, 0)),
        out_shape=jax.ShapeDtypeStruct((batch * seq, W), BF16),
        scratch_shapes=[pltpu.VMEM((n, W // LANES, tm, LANES), F32), pltpu.VMEM((n, 1, tm, LANES), F32)],
        compiler_params=_cparams(("arbitrary",), 32),
        name="merge_dilated",
    )(*outs, *lses, expand)


def _mm(a, b):
    return jnp.dot(a.astype(BF16), b.astype(BF16), preferred_element_type=F32)


def _dn_kernel(pb_ref, cw_ref, gpar_ref, ng_ref, lvl_ref, o_ref, xbuf, state, qs, ks, vs, gbs, obuf, *, tb):
    C = DN_CHUNK
    i = pl.program_id(1)

    @pl.when(i == 0)
    def _():
        xbuf[0:SUBLANES, :] = jnp.zeros((SUBLANES, CONV_CH), F32)
        state[...] = jnp.zeros_like(state)

    xbuf[SUBLANES:SUBLANES + tb, :] = pb_ref[0, :, 0:CONV_CH]
    y = cw_ref[CONV_K - 1:CONV_K, :] * xbuf[SUBLANES:SUBLANES + tb, :]
    for j in range(CONV_K - 1):
        off = SUBLANES - (CONV_K - 1) + j
        y = y + cw_ref[j:j + 1, :] * xbuf[off:off + tb, :]
    xbuf[0:SUBLANES, :] = xbuf[tb:tb + SUBLANES, :]
    qkv = _silu(y)

    def l2n(t):
        return t * lax.rsqrt(jnp.sum(t * t, axis=-1, keepdims=True) + EPS)

    for kh in range(H_K_B):
        sl = slice(kh * DK_B, (kh + 1) * DK_B)
        qs[:, sl] = l2n(qkv[:, sl]) * (DK_B ** -0.5)
        ks[:, sl] = l2n(qkv[:, BK_W + kh * DK_B:BK_W + (kh + 1) * DK_B])
    vs[...] = qkv[:, 2 * BK_W:]

    gl = pb_ref[0, :, PB_GATE_OFF:PB_GATE_OFF + LANES]
    lane = lax.broadcasted_iota(jnp.int32, (tb, LANES), 1)
    beta = jax.nn.sigmoid(gl)
    gdec = -jnp.exp(gpar_ref[0:1, :]) * jax.nn.softplus(gl + gpar_ref[1:2, :])
    gbs[...] = jnp.where(lane < H_V_B, beta, jnp.where(lane < 2 * H_V_B, gdec, 0.0))

    row = lax.broadcasted_iota(jnp.int32, (C, C), 0)
    col = lax.broadcasted_iota(jnp.int32, (C, C), 1)
    incl = row >= col
    strict = row > col
    tri = incl.astype(F32)
    eye = (row == col).astype(F32)
    heads = range(H_V_B)
    rep = H_V_B // H_K_B

    nc = tb // C
    kunits = [(ci, kh) for ci in range(nc) for kh in range(H_K_B)]
    units = [(ci, h) for ci in range(nc) for h in heads]
    rows = lambda ci: slice(ci * C, (ci + 1) * C)
    gb_c = [gbs[rows(ci), :] for ci in range(nc)]
    gc_c = [jnp.dot(tri, g, precision=HIGHEST, preferred_element_type=F32) for g in gb_c]
    gc_t = [g.T for g in gc_c]
    k_c = {(ci, kh): ks[rows(ci), kh * DK_B:(kh + 1) * DK_B] for ci, kh in kunits}
    q_c = {(ci, kh): qs[rows(ci), kh * DK_B:(kh + 1) * DK_B] for ci, kh in kunits}
    k_t = {u: k_c[u].T for u in kunits}
    gq = {u: _mm(jnp.concatenate([k_c[u], q_c[u]], axis=0), k_t[u]) for u in kunits}
    kof = lambda u: (u[0], u[1] // rep)
    beta_col = {(ci, h): gb_c[ci][:, h:h + 1] for ci, h in units}
    gcol = {(ci, h): gc_c[ci][:, H_V_B + h:H_V_B + h + 1] for ci, h in units}
    grow = {(ci, h): gc_t[ci][H_V_B + h:H_V_B + h + 1, :] for ci, h in units}
    decay = {u: jnp.exp(jnp.where(incl, gcol[u] - grow[u], -jnp.inf)) for u in units}
    lmb = {u: jnp.where(strict, gq[kof(u)][:C] * beta_col[u] * decay[u], 0.0).astype(BF16) for u in units}
    tinv = {u: eye - (lmb[u] * lvl_ref[0]).astype(F32) for u in units}
    for lvl in range(1, int(math.log2(C))):
        t16 = {u: tinv[u].astype(BF16) for u in units}
        a16 = {u: jnp.dot(t16[u], lmb[u] * lvl_ref[lvl], preferred_element_type=F32).astype(BF16) for u in units}
        tinv = {u: tinv[u] - jnp.dot(a16[u], t16[u], preferred_element_type=F32) for u in units}
    egc = {u: jnp.exp(gcol[u]) for u in units}
    rhs = {(ci, h): jnp.concatenate([vs[rows(ci), h * DV_B:(h + 1) * DV_B] * beta_col[ci, h],
                                     k_c[kof((ci, h))] * (beta_col[ci, h] * egc[ci, h])], axis=1)
           for ci, h in units}
    sol = {u: _mm(tinv[u], rhs[u]) for u in units}
    qk = {u: jnp.where(incl, gq[kof(u)][C:] * decay[u], 0.0).astype(BF16) for u in units}
    g_last = {u: grow[u][:, C - 1:C] for u in units}
    kd_t = {u: (k_t[kof(u)] * jnp.exp(g_last[u] - grow[u])).astype(BF16) for u in units}
    wq = {u: jnp.concatenate([sol[u][:, DV_B:], q_c[kof(u)] * egc[u]], axis=0).astype(BF16) for u in units}

    s_h = [state[h] for h in heads]
    for ci in range(nc):
        ws = [jnp.dot(wq[ci, h], s_h[h].astype(BF16), preferred_element_type=F32) for h in heads]
        v_new = [(sol[ci, h][:, :DV_B] - ws[h][:C]).astype(BF16) for h in heads]
        for h in heads:
            obuf[rows(ci), h * DV_B:(h + 1) * DV_B] = ws[h][C:] + jnp.dot(
                qk[ci, h], v_new[h], preferred_element_type=F32)
        s_h = [s_h[h] * jnp.exp(g_last[ci, h]) + jnp.dot(kd_t[ci, h], v_new[h], preferred_element_type=F32)
               for h in heads]
    for h in heads:
        state[h] = s_h[h]

    z = pb_ref[0, :, CONV_CH:CONV_CH + BV_W]
    ng = ng_ref[...]
    for hv in range(H_V_B):
        sl = slice(hv * DV_B, (hv + 1) * DV_B)
        o = obuf[:, sl]
        on = o * lax.rsqrt(jnp.mean(o * o, axis=-1, keepdims=True) + EPS) * ng
        o_ref[0, :, sl] = (on * _silu(z[:, sl])).astype(o_ref.dtype)


def gated_deltanet(pb, conv_w, a_log, dt_bias, norm_g, batch, seq):
    tb = 256
    view = pb.reshape(batch, seq, PB_W)
    gpar = jnp.zeros((2, LANES), F32)
    gpar = gpar.at[0, H_V_B:2 * H_V_B].set(a_log.astype(F32)).at[1, H_V_B:2 * H_V_B].set(dt_bias.astype(F32))
    n_lvl = int(math.log2(DN_CHUNK))
    ii = jnp.arange(DN_CHUNK)
    lvl_masks = jnp.stack([((ii[:, None] >> v) == (ii[None, :] >> v) + 1) & (((ii[:, None] >> v) & 1) == 1)
                           for v in range(n_lvl)]).astype(BF16)
    out = pl.pallas_call(
        functools.partial(_dn_kernel, tb=tb),
        grid=(batch, seq // tb),
        in_specs=[pl.BlockSpec((1, tb, PB_W), lambda b, i: (b, i, 0)),
                  pl.BlockSpec((CONV_K, CONV_CH), lambda b, i: (0, 0)),
                  pl.BlockSpec((2, LANES), lambda b, i: (0, 0)),
                  pl.BlockSpec((1, DV_B), lambda b, i: (0, 0)),
                  pl.BlockSpec((n_lvl, DN_CHUNK, DN_CHUNK), lambda b, i: (0, 0, 0))],
        out_specs=pl.BlockSpec((1, tb, BV_W), lambda b, i: (b, i, 0)),
        out_shape=jax.ShapeDtypeStruct((batch, seq, BV_W), BF16),
        scratch_shapes=[pltpu.VMEM((tb + SUBLANES, CONV_CH), F32),
                        pltpu.VMEM((H_V_B, DK_B, DV_B), F32),
                        pltpu.VMEM((tb, BK_W), F32),
                        pltpu.VMEM((tb, BK_W), F32),
                        pltpu.VMEM((tb, BV_W), F32),
                        pltpu.VMEM((tb, LANES), F32),
                        pltpu.VMEM((tb, BV_W), F32)],
        compiler_params=_cparams(("arbitrary", "arbitrary"), 48),
        name="gated_deltanet",
    )(view, conv_w, gpar, norm_g.reshape(1, DV_B).astype(F32), lvl_masks)
    return out.reshape(batch * seq, BV_W)


ROUTE_E1, ROUTE_E2, ROUTE_G1, ROUTE_G2 = 0, 1, 2, 3


def _pack_halves(hb):
    half = hb.shape[1] // 2
    lo = lax.bitcast_convert_type(hb[:, :half].astype(F32), jnp.uint32)
    hi = lax.bitcast_convert_type(hb[:, half:].astype(F32), jnp.uint32)
    return (hi & jnp.uint32(0xFFFF0000)) | (lo >> 16)


def _unpack_halves(w):
    lo = lax.bitcast_convert_type(w << 16, F32).astype(BF16)
    hi = lax.bitcast_convert_type(w & jnp.uint32(0xFFFF0000), F32).astype(BF16)
    return lo, hi


def _outproj_kernel(oa_ref, ob_ref, oc_ref, w_ref, x_ref, g1_ref, n2_ref, sc2_ref, sh2_ref, rw_ref, rb_ref,
                    x1_ref, hp_ref, route_ref, cnt_ref):
    i = pl.program_id(0)
    o = jnp.concatenate([oa_ref[...], ob_ref[...], oc_ref[...]], axis=-1)
    x1 = x_ref[...] + g1_ref[0] * jnp.dot(o, w_ref[0], preferred_element_type=F32)
    x1_ref[...] = x1
    h = x1 * lax.rsqrt(jnp.mean(x1 * x1, axis=-1, keepdims=True) + EPS) * n2_ref[...]
    h = h * (1.0 + sc2_ref[0]) + sh2_ref[0]
    h_hi = h.astype(BF16)
    hp_ref[...] = h_hi.astype(F32)

    h_lo = (h - h_hi.astype(F32)).astype(BF16)
    hw = jnp.dot(h_hi, rw_ref[0], preferred_element_type=F32)
    logits = hw[:, :LANES] + hw[:, LANES:] + jnp.dot(h_lo, rw_ref[0, :, 0:LANES], preferred_element_type=F32) \
        + rb_ref[...]
    tm = logits.shape[0]
    lane = lax.broadcasted_iota(jnp.int32, (tm, LANES), 1).astype(F32)
    big = float(LANES)
    glane = (lane >= N_EXPERTS) & (lane < N_EXPERTS + N_GROUPS)
    gmax = jnp.max(jnp.where(glane, logits, -jnp.inf), axis=-1, keepdims=True)
    gsel = jnp.min(jnp.where(glane & (logits == gmax), lane, big), axis=-1, keepdims=True) - N_EXPERTS
    pg = 1.0 / jnp.sum(jnp.where(glane, jnp.exp(logits - gmax), 0.0), axis=-1, keepdims=True)
    emask = (lane >= gsel * EXPERTS_PER_GROUP) & (lane < (gsel + 1.0) * EXPERTS_PER_GROUP)
    v1 = jnp.max(jnp.where(emask, logits, -jnp.inf), axis=-1, keepdims=True)
    i1 = jnp.min(jnp.where(emask & (logits == v1), lane, big), axis=-1, keepdims=True)
    m2 = emask & (lane != i1)
    v2 = jnp.max(jnp.where(m2, logits, -jnp.inf), axis=-1, keepdims=True)
    i2 = jnp.min(jnp.where(m2 & (logits == v2), lane, big), axis=-1, keepdims=True)
    t = jnp.exp(v2 - v1)
    p1 = 1.0 / (1.0 + t)
    p2 = t / (1.0 + t)
    route_ref[...] = jnp.where(lane == ROUTE_E1, i1, jnp.where(lane == ROUTE_E2, i2, jnp.where(
        lane == ROUTE_G1, pg * p1, jnp.where(lane == ROUTE_G2, pg * p2, 0.0))))

    @pl.when(i == 0)
    def _():
        cnt_ref[...] = jnp.zeros_like(cnt_ref)

    onehot = ((lane == i1) | (lane == i2)).astype(F32)
    cnt_ref[...] += jnp.broadcast_to(jnp.sum(onehot, axis=0, keepdims=True), cnt_ref.shape)


def output_projection(oa, ob, oc, w_out, x, mod, norm2_g, rw2, rb, layer, seq):
    T, D = x.shape
    tm = 256
    per_b = seq // tm
    modspec = lambda col: pl.BlockSpec((1, 1, D), lambda i: (i // per_b, 0, col))
    return pl.pallas_call(
        _outproj_kernel,
        grid=(T // tm,),
        in_specs=[pl.BlockSpec((tm, A_W), lambda i: (i, 0)),
                  pl.BlockSpec((tm, BV_W), lambda i: (i, 0)),
                  pl.BlockSpec((tm, C_Q_W), lambda i: (i, 0)),
                  pl.BlockSpec((1, D, D), lambda i: (layer, 0, 0)),
                  pl.BlockSpec((tm, D), lambda i: (i, 0)),
                  modspec(2),
                  pl.BlockSpec((1, D), lambda i: (0, 0)),
                  modspec(4), modspec(3),
                  pl.BlockSpec((1, D, 2 * LANES), lambda i: (layer, 0, 0)),
                  pl.BlockSpec((1, LANES), lambda i: (0, 0))],
        out_specs=[pl.BlockSpec((tm, D), lambda i: (i, 0)),
                   pl.BlockSpec((tm, D), lambda i: (i, 0)),
                   pl.BlockSpec((tm, LANES), lambda i: (i, 0)),
                   pl.BlockSpec((SUBLANES, LANES), lambda i: (0, 0))],
        out_shape=[jax.ShapeDtypeStruct((T, D), F32),
                   jax.ShapeDtypeStruct((T, D), F32),
                   jax.ShapeDtypeStruct((T, LANES), F32),
                   jax.ShapeDtypeStruct((SUBLANES, LANES), F32)],
        compiler_params=_cparams(("arbitrary",), 48),
        name="output_projection_router",
    )(oa, ob, oc, w_out, x, mod, norm2_g.reshape(1, D), mod, mod, rw2, rb)


EXPERT_TILE = 256


def _plan_kernel(route_ref, cnt_ref, pos_ref, be_ref, carry, pstart, *, tt, nblk_lanes):
    i = pl.program_id(0)
    r128 = lax.broadcasted_iota(jnp.int32, (LANES, LANES), 0)
    c128 = lax.broadcasted_iota(jnp.int32, (LANES, LANES), 1)

    @pl.when(i == 0)
    def _():
        cnt = cnt_ref[...]
        padded = jnp.ceil(cnt / EXPERT_TILE) * EXPERT_TILE
        upper = (r128 <= c128).astype(F32)
        pend = jnp.dot(padded, upper, precision=HIGHEST, preferred_element_type=F32)
        pstart[...] = pend - padded
        carry[...] = jnp.zeros_like(carry)
        pend_col = jnp.broadcast_to(pend[0:1, :], (LANES, LANES)).T
        ecol = r128 < N_EXPERTS
        n_used = pend[0:1, N_EXPERTS - 1:N_EXPERTS] / EXPERT_TILE
        for v in range(nblk_lanes // LANES):
            jlane = c128[0:1, :] + v * LANES
            jrow = ((c128 + v * LANES) * EXPERT_TILE).astype(F32)
            n = jnp.sum(jnp.where(ecol & (pend_col <= jrow), 1.0, 0.0), axis=0, keepdims=True)
            be = jnp.where(jlane == nblk_lanes - 1, n_used, jnp.minimum(n, float(N_EXPERTS - 1)))
            be_ref[:, v * LANES:(v + 1) * LANES] = jnp.broadcast_to(be.astype(jnp.int32), (SUBLANES, LANES))

    route = route_ref[...]
    lane = lax.broadcasted_iota(jnp.int32, (tt, LANES), 1).astype(F32)
    oh1 = lane == route[:, ROUTE_E1:ROUTE_E1 + 1]
    oh2 = lane == route[:, ROUTE_E2:ROUTE_E2 + 1]
    both = oh1.astype(F32) + oh2.astype(F32)
    rr = lax.broadcasted_iota(jnp.int32, (tt, tt), 0)
    cc = lax.broadcasted_iota(jnp.int32, (tt, tt), 1)
    before = jnp.dot((rr > cc).astype(BF16), both.astype(BF16), preferred_element_type=F32) + carry[0:1, :]
    base = before + pstart[0:1, :]
    p1 = jnp.sum(jnp.where(oh1, base, 0.0), axis=-1, keepdims=True)
    p2 = jnp.sum(jnp.where(oh2, base, 0.0), axis=-1, keepdims=True)
    pos_ref[...] = jnp.where(lane == 0.0, p1, jnp.where(lane == 1.0, p2, 0.0)).astype(jnp.int32)
    carry[...] += jnp.broadcast_to(jnp.sum(both, axis=0, keepdims=True), carry.shape)


def routing_plan(route, cnt, n_blocks):
    T = route.shape[0]
    tt = 512
    nblk_lanes = -(-n_blocks // LANES) * LANES
    return pl.pallas_call(
        functools.partial(_plan_kernel, tt=tt, nblk_lanes=nblk_lanes),
        grid=(T // tt,),
        in_specs=[pl.BlockSpec((tt, LANES), lambda i: (i, 0)),
                  pl.BlockSpec((SUBLANES, LANES), lambda i: (0, 0))],
        out_specs=[pl.BlockSpec((tt, LANES), lambda i: (i, 0)),
                   pl.BlockSpec((SUBLANES, nblk_lanes), lambda i: (0, 0))],
        out_shape=[jax.ShapeDtypeStruct((T, LANES), jnp.int32),
                   jax.ShapeDtypeStruct((SUBLANES, nblk_lanes), jnp.int32)],
        scratch_shapes=[pltpu.VMEM((SUBLANES, LANES), F32), pltpu.VMEM((SUBLANES, LANES), F32)],
        compiler_params=_cparams(("arbitrary",), 32),
        name="routing_plan",
    )(route, cnt)


def _row_copy(src_ref, dst_ref, src_row, dst_row, sem):
    return pltpu.make_async_copy(src_ref.at[pl.ds(src_row, 1)], dst_ref.at[pl.ds(dst_row, 1)], sem)


def _dispatch_kernel(pos_ref, hp_ref, xs_in_ref, xs_ref, sem, *, td):
    del xs_in_ref
    base = pl.program_id(0) * td

    def issue(g, c):
        r0 = pl.multiple_of(g * SUBLANES, SUBLANES)
        for u in range(SUBLANES):
            for k in range(TOP_K):
                _row_copy(hp_ref, xs_ref, r0 + u, pos_ref[(base + r0 + u) * TOP_K + k], sem).start(priority=k)
        return c

    lax.fori_loop(0, td // SUBLANES, issue, 0)

    def drain(r, c):
        for k in range(TOP_K):
            _row_copy(hp_ref, xs_ref, 0, 0, sem).wait()
        return c

    lax.fori_loop(0, td, drain, 0, unroll=8)


def dispatch_rows(pos_flat, hp, n_rows):
    T, W = hp.shape
    td = 512
    return pl.pallas_call(
        functools.partial(_dispatch_kernel, td=td),
        grid_spec=pltpu.PrefetchScalarGridSpec(
            num_scalar_prefetch=1, grid=(T // td,),
            in_specs=[pl.BlockSpec((td, W), lambda i, pos: (i, 0)),
                      pl.BlockSpec(memory_space=pl.ANY)],
            out_specs=pl.BlockSpec(memory_space=pl.ANY),
            scratch_shapes=[pltpu.SemaphoreType.DMA(())]),
        out_shape=jax.ShapeDtypeStruct((n_rows, W), hp.dtype),
        input_output_aliases={2: 0},
        compiler_params=_cparams(("arbitrary",), 32),
        name="moe_dispatch",
    )(pos_flat, hp, jnp.zeros((n_rows, W), hp.dtype))


def _expert_kernel(be_ref, xs_ref, wg_hbm, wu_hbm, wd_hbm, y_ref, wgf, wuf, wdf, wgb, wub, wdb, seg_ref, sems,
                   *, layer):
    j = pl.program_id(0)
    n_used = be_ref[be_ref.shape[0] - 1]

    def weight_copies(expert, slot):
        return [pltpu.make_async_copy(src.at[layer, expert], dst.at[slot], sems.at[slot, t])
                for t, (src, dst) in enumerate(((wg_hbm, wgf), (wu_hbm, wuf), (wd_hbm, wdf)))]

    @pl.when(j == 0)
    def _():
        seg_ref[0] = 0
        for cp in weight_copies(be_ref[0], 0):
            cp.start()

    @pl.when(j < n_used)
    def _():
        e = be_ref[j]

        @pl.when((j == 0) | (e != be_ref[jnp.maximum(j - 1, 0)]))
        def _():
            slot = lax.rem(seg_ref[0], 2)
            seg_ref[0] = seg_ref[0] + 1
            nxt = lax.while_loop(lambda t: (t < n_used) & (be_ref[jnp.minimum(t, n_used - 1)] == e),
                                 lambda t: t + 1, j + 1)

            @pl.when(nxt < n_used)
            def _():
                for cp in weight_copies(be_ref[jnp.minimum(nxt, n_used - 1)], 1 - slot):
                    cp.start()

            for cp in weight_copies(e, slot):
                cp.wait()
            wgb[...] = wgf[slot].astype(BF16)
            wub[...] = wuf[slot].astype(BF16)
            wdb[...] = wdf[slot].astype(BF16)

        xb = xs_ref[...].astype(BF16)
        gate = jnp.dot(xb, wgb[...], preferred_element_type=F32)
        up = jnp.dot(xb, wub[...], preferred_element_type=F32)
        hid = (_silu(gate) * up).astype(BF16)
        y_ref[...] = jnp.dot(hid, wdb[...], preferred_element_type=F32)

    @pl.when(j >= n_used)
    def _():
        y_ref[...] = jnp.zeros_like(y_ref)


def expert_mlp(block_plan, xs, w_gate, w_up, w_down, layer):
    n_rows, W = xs.shape
    _, E, D, F = w_gate.shape
    n_blocks = n_rows // EXPERT_TILE
    assert n_blocks < block_plan.shape[0]
    last = block_plan.shape[0] - 1
    used = lambda j, be: jnp.minimum(j, be[last] - 1)
    hbm = pl.BlockSpec(memory_space=pl.ANY)
    return pl.pallas_call(
        functools.partial(_expert_kernel, layer=layer),
        grid_spec=pltpu.PrefetchScalarGridSpec(
            num_scalar_prefetch=1, grid=(n_blocks,),
            in_specs=[pl.BlockSpec((EXPERT_TILE, W), lambda j, be: (used(j, be), 0)), hbm, hbm, hbm],
            out_specs=pl.BlockSpec((EXPERT_TILE, D), lambda j, be: (j, 0)),
            scratch_shapes=[pltpu.VMEM((2, D, F), F32), pltpu.VMEM((2, D, F), F32), pltpu.VMEM((2, F, D), F32),
                            pltpu.VMEM((D, F), BF16), pltpu.VMEM((D, F), BF16), pltpu.VMEM((F, D), BF16),
                            pltpu.SMEM((1,), jnp.int32), pltpu.SemaphoreType.DMA((2, 3))]),
        out_shape=jax.ShapeDtypeStruct((n_rows, D), F32),
        compiler_params=_cparams(("arbitrary",), 56),
        name="expert_mlp",
    )(block_plan, xs, w_gate, w_up, w_down)


def _combine_kernel(pos_ref, y_ref, x1_ref, g2_ref, route_ref, fg_ref, out_ref, ybuf, sem, *, tc, final_norm):
    base = pl.program_id(0) * tc

    def issue(g, c):
        r0 = pl.multiple_of(g * SUBLANES, SUBLANES)
        for u in range(SUBLANES):
            for k in range(TOP_K):
                _row_copy(y_ref, ybuf.at[k], pos_ref[(base + r0 + u) * TOP_K + k], r0 + u, sem).start(priority=k)
        return c

    lax.fori_loop(0, tc // SUBLANES, issue, 0)

    def drain(r, c):
        for k in range(TOP_K):
            _row_copy(y_ref, ybuf.at[k], 0, 0, sem).wait()
        return c

    lax.fori_loop(0, tc, drain, 0, unroll=8)

    route = route_ref[...]
    moe = route[:, ROUTE_G1:ROUTE_G1 + 1] * ybuf[0] + route[:, ROUTE_G2:ROUTE_G2 + 1] * ybuf[1]
    x2 = x1_ref[...] + g2_ref[0] * moe
    if final_norm:
        x2 = x2 * lax.rsqrt(jnp.mean(x2 * x2, axis=-1, keepdims=True) + EPS) * fg_ref[...]
    out_ref[...] = x2


def combine_rows(pos_flat, y, x1, mod, route, final_g, seq, final_norm):
    T, D = x1.shape
    tc = 512
    per_b = seq // tc
    return pl.pallas_call(
        functools.partial(_combine_kernel, tc=tc, final_norm=final_norm),
        grid_spec=pltpu.PrefetchScalarGridSpec(
            num_scalar_prefetch=1, grid=(T // tc,),
            in_specs=[pl.BlockSpec(memory_space=pl.ANY),
                      pl.BlockSpec((tc, D), lambda i, pos: (i, 0)),
                      pl.BlockSpec((1, 1, D), lambda i, pos: (i // per_b, 0, 5)),
                      pl.BlockSpec((tc, LANES), lambda i, pos: (i, 0)),
                      pl.BlockSpec((1, D), lambda i, pos: (0, 0))],
            out_specs=pl.BlockSpec((tc, D), lambda i, pos: (i, 0)),
            scratch_shapes=[pltpu.VMEM((TOP_K, tc, D), F32), pltpu.SemaphoreType.DMA(())]),
        out_shape=jax.ShapeDtypeStruct((T, D), F32),
        compiler_params=_cparams(("arbitrary",), 40),
        name="moe_combine",
    )(pos_flat, y, x1, mod, route, final_g.reshape(1, D))


def _alibi_slopes(n):
    return 2.0 ** (-8.0 * jnp.arange(1, n + 1, dtype=F32) / n)


def _permute_w_in(w_in):
    L, D, _ = w_in.shape
    g_end = PA_W + PB_GATE_OFF + 2 * H_V_B
    pad = jnp.zeros((L, D, LANES - 2 * H_V_B), w_in.dtype)
    cq = w_in[:, :, g_end:g_end + C_Q_W]
    ck = w_in[:, :, g_end + C_Q_W:g_end + C_Q_W + C_KV_W].reshape(L, D, HKV_C, DH_C)
    cv = w_in[:, :, g_end + C_Q_W + C_KV_W:].reshape(L, D, HKV_C, DH_C)
    dup = lambda t: jnp.concatenate([t, t], axis=-1).reshape(L, D, PC_KV_W)
    return jnp.concatenate([w_in[:, :, :g_end], pad, cq, dup(ck), dup(cv)], axis=-1).astype(BF16)


def kernel(x, c, norm1_g, norm2_g, ada_w, ada_b, w_in, dn_conv_w, dn_a_log, dn_dt_bias, dn_norm_g, attn_sinks,
           w_out, router_group_w, router_group_b, router_expert_w, router_expert_b, expert_w_gate, expert_w_up,
           expert_w_down, final_norm_g):
    B, S, D = x.shape
    L = ada_w.shape[0]
    T = B * S
    n_rows = T * TOP_K + N_EXPERTS * EXPERT_TILE
    n_blocks = n_rows // EXPERT_TILE

    mod_all = ada_modulation(c, ada_w, ada_b)[:, :B].reshape(L, B, 1, 6 * D)
    w_perm = _permute_w_in(w_in)
    w_out_b = w_out.astype(BF16)
    rw = jnp.concatenate([router_expert_w, router_group_w,
                          jnp.zeros((L, D, LANES - N_EXPERTS - N_GROUPS), F32)], axis=-1)
    rw_hi = rw.astype(BF16)
    rw2 = jnp.concatenate([rw_hi, (rw - rw_hi.astype(F32)).astype(BF16)], axis=-1)
    rb = jnp.concatenate([router_expert_b, router_group_b,
                          jnp.zeros((L, LANES - N_EXPERTS - N_GROUPS), F32)], axis=-1).reshape(L, 1, LANES)
    slopes_a = _alibi_slopes(H_A)
    bias_a = [_band_bias(slopes_a, float(dil), wdw // dil) for (wdw, dil) in DILATED_PATTERNS]
    bias_c = _band_bias(_alibi_slopes(H_C), 1.0, SWA_WINDOW - 1)

    dils = tuple(dil for (_, dil) in DILATED_PATTERNS)
    xt = x.reshape(T, D)
    for l in range(L):
        mod = mod_all[l]
        *pas, pb, pc = input_projection(xt, norm1_g[l], mod, w_perm, l, B, S, dils)
        outs, lses = [], []
        for pa_d, (wdw, dil), bias in zip(pas, DILATED_PATTERNS, bias_a):
            o, lse = banded_attention(pa_d, B, S, dil, n_pairs=H_A // 2, pairs_per_kv=1, dh=DH_A, q_off=0,
                                      k_off=A_W, v_off=2 * A_W, width=PA_W, bias=bias, want_lse=True)
            outs.append(o)
            lses.append(lse)
        o_a = merge_dilated(outs, lses, B, S, dils)
        o_b = gated_deltanet(pb, dn_conv_w[l], dn_a_log[l], dn_dt_bias[l], dn_norm_g[l], B, S)
        o_c = banded_attention(pc.reshape(B, S, PC_W), B, S, 1, n_pairs=H_C // 2, pairs_per_kv=H_C // HKV_C // 2,
                               dh=DH_C, q_off=0, k_off=C_Q_W, v_off=C_Q_W + PC_KV_W, width=PC_W, bias=bias_c,
                               sinks=attn_sinks[l]).reshape(T, C_Q_W)
        x1, hp, route, cnt = output_projection(o_a, o_b, o_c, w_out_b, xt, mod, norm2_g[l], rw2, rb[l], l, S)
        pos, be = routing_plan(route, cnt, n_blocks)
        pos_flat = pos[:, :TOP_K].reshape(T * TOP_K)
        xs = dispatch_rows(pos_flat, hp, n_rows)
        y = expert_mlp(be[0], xs, expert_w_gate, expert_w_up, expert_w_down, l)
        xt = combine_rows(pos_flat, y, x1, mod, route, final_norm_g, S, final_norm=(l == L - 1))
    return xt.reshape(B, S, D)
```
